```python
import math
import jax
import jax.numpy as jnp
from jax import lax
import numpy as np

D_MODEL = 2048
BATCH = 4
SEQ = 2048
DEPTH = 4
DEC_BATCH = 32
DEC_SEQ = 4
PAST_LEN = 16384
PAGE_SIZE = 128

N_MIXERS = 3
N_LAYERS_A = (DEPTH + 2) // 3
N_LAYERS_B = (DEPTH + 1) // 3
N_LAYERS_C = DEPTH // 3
NORM_EPS = 1e-6

NUM_BUCKETS = 32
MAX_DISTANCE = 128

HEAD_DIM_A = 64
N_HEADS_A = D_MODEL // (2 * HEAD_DIM_A)
N_KV_HEADS_A = 4
GROUP_A = N_HEADS_A // N_KV_HEADS_A
DKV_A = 2 * HEAD_DIM_A
DV_A = 2 * HEAD_DIM_A
SCALE_A = HEAD_DIM_A ** -0.5
QBLOCK = 128
A_IN = N_HEADS_A * 2 * HEAD_DIM_A + N_KV_HEADS_A * DKV_A + N_KV_HEADS_A * DV_A

EXPAND_B = 128
N_HEADS_B = D_MODEL // EXPAND_B
DK_B = EXPAND_B
DV_B = D_MODEL // N_HEADS_B
CHUNK_B = 64
B_IN = 2 * N_HEADS_B * DK_B + 2 * N_HEADS_B * DV_B

HEAD_DIM_C = 64
N_HEADS_C = D_MODEL // HEAD_DIM_C
N_KV_HEADS_C = 4
GROUP_C = N_HEADS_C // N_KV_HEADS_C
WINDOW = 128
SCALE_C = HEAD_DIM_C ** -0.5
C_IN = N_HEADS_C * HEAD_DIM_C + 2 * N_KV_HEADS_C * HEAD_DIM_C

N_BIAS_COLS = 2 * N_HEADS_A
D_FF = -(-8 * D_MODEL // (3 * 256)) * 256

kernel_name = "hybrid_diffattn_hgrn2_swa_decode_step"


def win_buf():
    return min(WINDOW, PAST_LEN)


def n_pool():
    used = DEC_BATCH * (PAST_LEN // PAGE_SIZE)
    return used + max(1, used // 4)


def rmsnorm(x, g):
    xf = x.astype(jnp.float32)
    y = xf * lax.rsqrt(jnp.mean(xf * xf, axis=-1, keepdims=True) + NORM_EPS)
    return (y * g.astype(jnp.float32)).astype(x.dtype)


def rel_bias(dist, table):
    n = jnp.maximum(dist, 0)
    max_exact = NUM_BUCKETS // 2
    log_ratio = jnp.log(jnp.maximum(n, 1).astype(jnp.float32) / max_exact) / math.log(MAX_DISTANCE / max_exact)
    large = jnp.minimum(max_exact + (log_ratio * (NUM_BUCKETS - max_exact)).astype(jnp.int32), NUM_BUCKETS - 1)
    bucket = jnp.where(n < max_exact, n, large)
    return jnp.take(table, bucket, axis=0).astype(jnp.float32)


def swiglu(x, w_gate_up, w_down):
    gate, up = jnp.split(x @ w_gate_up, 2, axis=-1)
    return (jax.nn.silu(gate) * up) @ w_down


def diff_lambda(lam_p, lam_init):
    lp = lam_p.astype(jnp.float32)
    return jnp.exp(jnp.sum(lp[0] * lp[1])) - jnp.exp(jnp.sum(lp[2] * lp[3])) + lam_init


def diff_project(x, w_in):
    b, t = x.shape[0], x.shape[1]
    h = x @ w_in
    nq = N_HEADS_A * 2 * HEAD_DIM_A
    nk = N_KV_HEADS_A * DKV_A
    q = h[..., :nq].reshape(b, t, N_KV_HEADS_A, GROUP_A, 2, HEAD_DIM_A)
    k = h[..., nq:nq + nk].reshape(b, t, N_KV_HEADS_A, 2, HEAD_DIM_A)
    v = h[..., nq + nk:].reshape(b, t, N_KV_HEADS_A, DV_A)
    return q, k, v


def diff_weights(s, dist, lam, table):
    nq, nk = dist.shape
    bias = rel_bias(dist, table).reshape(nq, nk, N_KV_HEADS_A, GROUP_A, 2).transpose(2, 3, 4, 0, 1)
    s = jnp.where(dist >= 0, s * SCALE_A + bias, -jnp.inf)
    p = jax.nn.softmax(s, axis=-1)
    return p[:, :, :, 0] - lam * p[:, :, :, 1]


def diff_output(o, subln_g, lam_init, w_out, dtype):
    b, t = o.shape[0], o.shape[1]
    o = rmsnorm(o, subln_g) * (1.0 - lam_init)
    return o.reshape(b, t, N_HEADS_A * DV_A).astype(dtype) @ w_out


def diff_attn_prompt(x, w_in, w_out, lam_p, subln_g, lam_init, table):
    b, s_len = x.shape[0], x.shape[1]
    q, k, v = diff_project(x, w_in)
    lam = diff_lambda(lam_p, lam_init)
    nb = s_len // QBLOCK
    q_blocks = q.reshape(b, nb, QBLOCK, N_KV_HEADS_A, GROUP_A, 2, HEAD_DIM_A).swapaxes(0, 1)
    vf = v.astype(jnp.float32)
    kpos = jnp.arange(s_len)

    def one_block(args):
        qb, i = args
        qpos = i * QBLOCK + jnp.arange(QBLOCK)
        s = jnp.einsum('bqhgmd,bkhmd->bhgmqk', qb, k, preferred_element_type=jnp.float32)
        a = diff_weights(s, qpos[:, None] - kpos[None, :], lam, table)
        return jnp.einsum('bhgqk,bkhe->bqhge', a, vf)

    o = lax.map(one_block, (q_blocks, jnp.arange(nb)))
    o = o.swapaxes(0, 1).reshape(b, s_len, N_KV_HEADS_A, GROUP_A, DV_A)
    y = diff_output(o, subln_g, lam_init, w_out, x.dtype)
    rows = jnp.stack([k.reshape(b, s_len, N_KV_HEADS_A, DKV_A), v], axis=2)
    return y, rows


def diff_attn_sample(x, past_pages, w_in, w_out, lam_p, subln_g, lam_init, table):
    b, t = x.shape[0], x.shape[1]
    q, k, v = diff_project(x, w_in)
    lam = diff_lambda(lam_p, lam_init)
    past = past_pages.reshape(b, -1, 2, N_KV_HEADS_A, DKV_A)
    p_len = past.shape[1]
    k_past = past[:, :, 0].reshape(b, p_len, N_KV_HEADS_A, 2, HEAD_DIM_A)
    v_past = past[:, :, 1]
    s = jnp.concatenate([
        jnp.einsum('bqhgmd,bkhmd->bhgmqk', q, k_past, preferred_element_type=jnp.float32),
        jnp.einsum('bqhgmd,bkhmd->bhgmqk', q, k, preferred_element_type=jnp.float32)], axis=-1)
    qpos = p_len + jnp.arange(t)
    kpos = jnp.arange(p_len + t)
    a = diff_weights(s, qpos[:, None] - kpos[None, :], lam, table)
    o = (jnp.einsum('bhgqk,bkhe->bqhge', a[..., :p_len], v_past.astype(jnp.float32))
         + jnp.einsum('bhgqk,bkhe->bqhge', a[..., p_len:], v.astype(jnp.float32)))
    y = diff_output(o, subln_g, lam_init, w_out, x.dtype)
    rows = jnp.stack([k.reshape(b, t, N_KV_HEADS_A, DKV_A), v], axis=2)
    return y, rows


def hgrn2_project(x, w_in, lb):
    b, t = x.shape[0], x.shape[1]
    h = (x @ w_in).astype(jnp.float32)
    nk = N_HEADS_B * DK_B
    nv = N_HEADS_B * DV_B
    q = jax.nn.silu(h[..., :nk])
    f = lb + (1.0 - lb) * jax.nn.sigmoid(h[..., nk:2 * nk])
    i = h[..., 2 * nk:2 * nk + nv]
    gate = h[..., 2 * nk + nv:]
    q = q.reshape(b, t, N_HEADS_B, DK_B)
    k = (1.0 - f).reshape(b, t, N_HEADS_B, DK_B)
    g = jnp.log(f).reshape(b, t, N_HEADS_B, DK_B)
    i = i.reshape(b, t, N_HEADS_B, DV_B)
    return q, k, i, g, gate


def gla_chunk(s0, q, k, v, g):
    c = q.shape[1]
    b_cum = jnp.cumsum(g, axis=1)
    o_inter = jnp.einsum('bthk,bhkv->bthv', q * jnp.exp(b_cum), s0)
    causal = jnp.tril(jnp.ones((c, c), dtype=bool))
    diff = b_cum[:, :, None] - b_cum[:, None, :]
    decay = jnp.exp(jnp.where(causal[None, :, :, None, None], diff, -jnp.inf))
    scores = jnp.einsum('bthk,btshk,bshk->bhts', q, decay, k)
    o = o_inter + jnp.einsum('bhts,bshv->bthv', scores, v)
    b_last = b_cum[:, -1]
    k_dec = k * jnp.exp(b_last[:, None] - b_cum)
    s_new = jnp.exp(b_last)[..., None] * s0 + jnp.einsum('bshk,bshv->bhkv', k_dec, v)
    return o, s_new


def hgrn2_output(o, gate, norm_g, w_out, dtype):
    b, t = o.shape[0], o.shape[1]
    o = rmsnorm(o, norm_g).reshape(b, t, N_HEADS_B * DV_B) * jax.nn.silu(gate)
    return o.astype(dtype) @ w_out


def hgrn2_prompt(x, w_in, w_out, lb, norm_g):
    b, s_len = x.shape[0], x.shape[1]
    q, k, v, g, gate = hgrn2_project(x, w_in, lb)
    nc = s_len // CHUNK_B

    def chunks(z):
        return z.reshape(b, nc, CHUNK_B, z.shape[2], z.shape[3]).swapaxes(0, 1)

    def step(state, inp):
        o, state = gla_chunk(state, inp[0], inp[1], inp[2], inp[3])
        return state, o

    s0 = jnp.zeros((b, N_HEADS_B, DK_B, DV_B), jnp.float32)
    s_fin, o = lax.scan(step, s0, (chunks(q), chunks(k), chunks(v), chunks(g)))
    o = o.swapaxes(0, 1).reshape(b, s_len, N_HEADS_B, DV_B)
    return hgrn2_output(o, gate, norm_g, w_out, x.dtype), s_fin


def hgrn2_sample(x, state, w_in, w_out, lb, norm_g):
    q, k, v, g, gate = hgrn2_project(x, w_in, lb)
    o, s_new = gla_chunk(state.astype(jnp.float32), q, k, v, g)
    return hgrn2_output(o, gate, norm_g, w_out, x.dtype), s_new


def swa_project(x, w_in):
    b, t = x.shape[0], x.shape[1]
    h = x @ w_in
    nq = N_HEADS_C * HEAD_DIM_C
    nk = N_KV_HEADS_C * HEAD_DIM_C
    q = h[..., :nq].reshape(b, t, N_KV_HEADS_C, GROUP_C, HEAD_DIM_C)
    k = h[..., nq:nq + nk].reshape(b, t, N_KV_HEADS_C, HEAD_DIM_C)
    v = h[..., nq + nk:].reshape(b, t, N_KV_HEADS_C, HEAD_DIM_C)
    return q, k, v


def swa_weights(s, dist, valid, sinks, table):
    nq, nk = dist.shape
    bias = rel_bias(dist, table).reshape(nq, nk, N_KV_HEADS_C, GROUP_C).transpose(2, 3, 0, 1)
    s = jnp.where(valid, s * SCALE_C + bias, -jnp.inf)
    sink = jnp.broadcast_to(sinks.astype(jnp.float32).reshape(N_KV_HEADS_C, GROUP_C, 1, 1), s.shape[:-1] + (1,))
    return jax.nn.softmax(jnp.concatenate([s, sink], axis=-1), axis=-1)[..., :-1]


def swa_output(o, w_out, dtype):
    b, t = o.shape[0], o.shape[1]
    return o.reshape(b, t, N_HEADS_C * HEAD_DIM_C).astype(dtype) @ w_out


def swa_prompt(x, w_in, w_out, sinks, table):
    b, s_len = x.shape[0], x.shape[1]
    q, k, v = swa_project(x, w_in)
    nb = s_len // WINDOW

    def band(z):
        zp = jnp.pad(z, ((0, 0), (WINDOW, 0), (0, 0), (0, 0)))
        prev = zp[:, :s_len].reshape(b, nb, WINDOW, z.shape[2], z.shape[3])
        return jnp.concatenate([prev, z.reshape(b, nb, WINDOW, z.shape[2], z.shape[3])], axis=2)

    qb = q.reshape(b, nb, WINDOW, N_KV_HEADS_C, GROUP_C, HEAD_DIM_C)
    kb, vb = band(k), band(v)
    s = jnp.einsum('bnqhgd,bnkhd->bnhgqk', qb, kb, preferred_element_type=jnp.float32)
    qoff = jnp.arange(WINDOW)
    koff = jnp.arange(2 * WINDOW)
    dist = qoff[:, None] + WINDOW - koff[None, :]
    kpos = jnp.arange(nb)[:, None] * WINDOW - WINDOW + koff[None, :]
    valid = (dist >= 0) & (dist <= WINDOW) & (kpos >= 0)[:, None, :]
    p = swa_weights(s, dist, valid[:, None, None], sinks, table)
    o = jnp.einsum('bnhgqk,bnkhd->bnqhgd', p, vb.astype(jnp.float32))
    o = o.reshape(b, s_len, N_KV_HEADS_C, GROUP_C, HEAD_DIM_C)
    wb = win_buf()
    rows = jnp.stack([k[:, s_len - wb:], v[:, s_len - wb:]], axis=2)
    return swa_output(o, w_out, x.dtype), rows


def swa_sample(x, buf, w_in, w_out, sinks, table):
    b, t = x.shape[0], x.shape[1]
    q, k, v = swa_project(x, w_in)
    wb = buf.shape[1]
    k_all = jnp.concatenate([buf[:, :, 0], k], axis=1)
    v_all = jnp.concatenate([buf[:, :, 1], v], axis=1)
    s = jnp.einsum('bqhgd,bkhd->bhgqk', q, k_all, preferred_element_type=jnp.float32)
    qpos = PAST_LEN + jnp.arange(t)
    kpos = PAST_LEN - wb + jnp.arange(wb + t)
    dist = qpos[:, None] - kpos[None, :]
    valid = (dist >= 0) & (dist <= WINDOW)
    p = swa_weights(s, dist, valid, sinks, table)
    o = jnp.einsum('bhgqk,bkhd->bqhgd', p, v_all.astype(jnp.float32))
    rows = jnp.stack([k_all[:, t:], v_all[:, t:]], axis=2)
    return swa_output(o, w_out, x.dtype), rows


def setup_inputs(seed: int = 0) -> dict:
    key = jax.random.key(seed)
    ks = jax.random.split(key, 24)
    d = D_MODEL
    n_pages = PAST_LEN // PAGE_SIZE
    pool = n_pool()
    page_table = jax.random.permutation(ks[5], pool)[:DEC_BATCH * n_pages].reshape(DEC_BATCH, n_pages).astype(jnp.int32)
    nrm = jax.random.normal
    return {
        'x_prompt': nrm(ks[0], (BATCH, SEQ, d), jnp.float32),
        'x_sample': nrm(ks[1], (DEC_BATCH, DEC_SEQ, d), jnp.float32),
        'cache_kv_a': nrm(ks[2], (N_LAYERS_A, pool, PAGE_SIZE, 2, N_KV_HEADS_A, DKV_A), jnp.float32),
        'state_hgrn_b': 0.5 * nrm(ks[3], (N_LAYERS_B, DEC_BATCH, N_HEADS_B, DK_B, DV_B), jnp.float32),
        'cache_win_c': nrm(ks[4], (N_LAYERS_C, DEC_BATCH, win_buf(), 2, N_KV_HEADS_C, HEAD_DIM_C), jnp.float32),
        'page_table': page_table,
        'norm_g': 1.0 + 0.05 * nrm(ks[6], (DEPTH, 4, d), jnp.float32),
        'a_w_in': nrm(ks[7], (N_LAYERS_A, d, A_IN), jnp.float32) * d ** -0.5,
        'a_w_out': nrm(ks[8], (N_LAYERS_A, N_HEADS_A * DV_A, d), jnp.float32) * (N_HEADS_A * DV_A) ** -0.5,
        'a_lambda': 0.1 * nrm(ks[9], (N_LAYERS_A, 4, HEAD_DIM_A), jnp.float32),
        'a_subln_g': 1.0 + 0.05 * nrm(ks[10], (N_LAYERS_A, DV_A), jnp.float32),
        'b_w_in': nrm(ks[11], (N_LAYERS_B, d, B_IN), jnp.float32) * d ** -0.5,
        'b_w_out': nrm(ks[12], (N_LAYERS_B, N_HEADS_B * DV_B, d), jnp.float32) * (N_HEADS_B * DV_B) ** -0.5,
        'b_lower_bound': nrm(ks[13], (DEPTH, N_HEADS_B * DK_B), jnp.float32),
        'b_norm_g': 1.0 + 0.05 * nrm(ks[14], (N_LAYERS_B, DV_B), jnp.float32),
        'c_w_in': nrm(ks[15], (N_LAYERS_C, d, C_IN), jnp.float32) * d ** -0.5,
        'c_w_out': nrm(ks[16], (N_LAYERS_C, N_HEADS_C * HEAD_DIM_C, d), jnp.float32) * (N_HEADS_C * HEAD_DIM_C) ** -0.5,
        'c_sinks': 0.5 * nrm(ks[17], (N_LAYERS_C, N_HEADS_C), jnp.float32),
        'rel_bias_table': 0.5 * nrm(ks[18], (NUM_BUCKETS, N_BIAS_COLS), jnp.float32),
        'ffn_w_gate_up': nrm(ks[19], (DEPTH, d, 2 * D_FF), jnp.float32) * d ** -0.5,
        'ffn_w_down': nrm(ks[20], (DEPTH, D_FF, d), jnp.float32) * D_FF ** -0.5,
    }


def reference(x_prompt, x_sample, cache_kv_a, state_hgrn_b, cache_win_c, page_table,
              norm_g, a_w_in, a_w_out, a_lambda, a_subln_g,
              b_w_in, b_w_out, b_lower_bound, b_norm_g,
              c_w_in, c_w_out, c_sinks, rel_bias_table, ffn_w_gate_up, ffn_w_down):
    sm = jax.nn.softmax(b_lower_bound.astype(jnp.float32), axis=0)
    lower_bounds = jnp.cumsum(sm, axis=0) - sm[0]
    hp, hs = x_prompt, x_sample
    kv_p, kv_s, hg_p, hg_s, win_p, win_s = [], [], [], [], [], []
    for layer in range(DEPTH):
        kind, j = layer % N_MIXERS, layer // N_MIXERS
        g = norm_g[layer]
        up, us = rmsnorm(hp, g[0]), rmsnorm(hs, g[0])
        if kind == 0:
            lam_init = 0.8 - 0.6 * math.exp(-0.3 * layer)
            mp, rp = diff_attn_prompt(up, a_w_in[j], a_w_out[j], a_lambda[j], a_subln_g[j], lam_init, rel_bias_table)
            past_pages = cache_kv_a[j, page_table]
            ms, rs = diff_attn_sample(us, past_pages, a_w_in[j], a_w_out[j], a_lambda[j], a_subln_g[j], lam_init, rel_bias_table)
            kv_p.append(rp)
            kv_s.append(rs)
        elif kind == 1:
            lb = lower_bounds[layer]
            mp, rp = hgrn2_prompt(up, b_w_in[j], b_w_out[j], lb, b_norm_g[j])
            ms, rs = hgrn2_sample(us, state_hgrn_b[j], b_w_in[j], b_w_out[j], lb, b_norm_g[j])
            hg_p.append(rp)
            hg_s.append(rs)
        else:
            mp, rp = swa_prompt(up, c_w_in[j], c_w_out[j], c_sinks[j], rel_bias_table)
            ms, rs = swa_sample(us, cache_win_c[j], c_w_in[j], c_w_out[j], c_sinks[j], rel_bias_table)
            win_p.append(rp)
            win_s.append(rs)
        hp = hp + rmsnorm(mp, g[1])
        hs = hs + rmsnorm(ms, g[1])
        hp = hp + rmsnorm(swiglu(rmsnorm(hp, g[2]), ffn_w_gate_up[layer], ffn_w_down[layer]), g[3])
        hs = hs + rmsnorm(swiglu(rmsnorm(hs, g[2]), ffn_w_gate_up[layer], ffn_w_down[layer]), g[3])
    return (hp, hs, jnp.stack(kv_p), jnp.stack(kv_s), jnp.stack(hg_p), jnp.stack(hg_s), jnp.stack(win_p), jnp.stack(win_s))
```

```python
import functools
import math

import numpy as np
import jax
import jax.numpy as jnp
from jax import lax
from jax.experimental import pallas as pl
from jax.experimental.pallas import tpu as pltpu

F32 = jnp.float32
BF16 = jnp.bfloat16

NORM_EPS = 1e-6
NUM_BUCKETS = 32
MAX_DISTANCE = 128
NEG = -1e30
LANES = 128
HALF = LANES // 2
MIB = 1024 * 1024

HEAD_DIM_A = 64
N_KV_A = 4
GROUP_A = 4
HEAD_DIM_C = 64
N_KV_C = 4
GROUP_C = 8
WINDOW = 128
TQ_A = 256
PAGES_PER_STEP = 8
CHUNK_B = 128
SUB_B = 16
ROWS_B = 512


def _cparams(sem, vmem_mib):
    return pltpu.CompilerParams(dimension_semantics=sem, vmem_limit_bytes=vmem_mib * MIB)


def _rms(x, g):
    ms = jnp.mean(x * x, axis=-1, keepdims=True)
    return x * lax.rsqrt(ms + NORM_EPS) * g


def _sigmoid(x):
    return 1.0 / (1.0 + jnp.exp(-x))


def _dot(a, b):
    return jnp.dot(a, b, preferred_element_type=F32)


def _dot_nt(a, b):
    return lax.dot_general(a, b, (((1,), (1,)), ((), ())), preferred_element_type=F32)


def _norm_matmul_kernel(x_ref, g_ref, w_ref, o_ref, xn_ref):
    @pl.when(pl.program_id(1) == 0)
    def _():
        xn_ref[...] = _rms(x_ref[...], g_ref[...]).astype(BF16)

    o_ref[...] = _dot(xn_ref[...], w_ref[...]).astype(o_ref.dtype)


def norm_matmul(x, g, w, *, tm, tn):
    m, k = x.shape
    n = w.shape[1]
    assert m % tm == 0 and n % tn == 0
    return pl.pallas_call(
        _norm_matmul_kernel,
        grid=(m // tm, n // tn),
        in_specs=[pl.BlockSpec((tm, k), lambda i, j: (i, 0)),
                  pl.BlockSpec((1, k), lambda i, j: (0, 0)),
                  pl.BlockSpec((k, tn), lambda i, j: (0, j))],
        out_specs=pl.BlockSpec((tm, tn), lambda i, j: (i, j)),
        out_shape=jax.ShapeDtypeStruct((m, n), F32),
        scratch_shapes=[pltpu.VMEM((tm, k), BF16)],
        compiler_params=_cparams(("parallel", "arbitrary"), 48),
        name="norm_matmul",
    )(x, g.reshape(1, k), w)


def _norm_swiglu_kernel(x_ref, g_ref, wg_ref, wu_ref, o_ref, xn_ref):
    @pl.when(pl.program_id(1) == 0)
    def _():
        xn_ref[...] = _rms(x_ref[...], g_ref[...]).astype(BF16)

    xn = xn_ref[...]
    gate = _dot(xn, wg_ref[...])
    up = _dot(xn, wu_ref[...])
    o_ref[...] = (gate * _sigmoid(gate) * up).astype(o_ref.dtype)


def norm_swiglu(x, g, w_gate_up, *, tm, tn):
    m, k = x.shape
    dff = w_gate_up.shape[1] // 2
    assert m % tm == 0 and dff % tn == 0
    nj = dff // tn
    return pl.pallas_call(
        _norm_swiglu_kernel,
        grid=(m // tm, nj),
        in_specs=[pl.BlockSpec((tm, k), lambda i, j: (i, 0)),
                  pl.BlockSpec((1, k), lambda i, j: (0, 0)),
                  pl.BlockSpec((k, tn), lambda i, j: (0, j)),
                  pl.BlockSpec((k, tn), lambda i, j: (0, j + nj))],
        out_specs=pl.BlockSpec((tm, tn), lambda i, j: (i, j)),
        out_shape=jax.ShapeDtypeStruct((m, dff), BF16),
        scratch_shapes=[pltpu.VMEM((tm, k), BF16)],
        compiler_params=_cparams(("parallel", "arbitrary"), 48),
        name="norm_swiglu",
    )(x, g.reshape(1, k), w_gate_up, w_gate_up)


def _matmul_postnorm_kernel(y_ref, w_ref, g_ref, h_ref, o_ref, acc_ref, *, nk):
    k = pl.program_id(1)
    part = _dot(y_ref[...], w_ref[...])

    @pl.when(k == 0)
    def _():
        acc_ref[...] = part

    @pl.when(k > 0)
    def _():
        acc_ref[...] += part

    @pl.when(k == nk - 1)
    def _():
        o_ref[...] = h_ref[...] + _rms(acc_ref[...], g_ref[...])


def matmul_postnorm_residual(y, w, g, h, *, tm, tk):
    m, kdim = y.shape
    n = w.shape[1]
    assert m % tm == 0 and kdim % tk == 0
    nk = kdim // tk
    return pl.pallas_call(
        functools.partial(_matmul_postnorm_kernel, nk=nk),
        grid=(m // tm, nk),
        in_specs=[pl.BlockSpec((tm, tk), lambda i, k: (i, k)),
                  pl.BlockSpec((tk, n), lambda i, k: (k, 0)),
                  pl.BlockSpec((1, n), lambda i, k: (0, 0)),
                  pl.BlockSpec((tm, n), lambda i, k: (i, 0))],
        out_specs=pl.BlockSpec((tm, n), lambda i, k: (i, 0)),
        out_shape=jax.ShapeDtypeStruct((m, n), F32),
        scratch_shapes=[pltpu.VMEM((tm, n), F32)],
        compiler_params=_cparams(("parallel", "arbitrary"), 48),
        name="matmul_postnorm_residual",
    )(y, w, g.reshape(1, n), h)


def _bucket_np(dist):
    n = np.maximum(dist, 0)
    max_exact = NUM_BUCKETS // 2
    ratio = np.log(np.maximum(n, 1).astype(np.float32) / np.float32(max_exact)) / np.float32(
        math.log(MAX_DISTANCE / max_exact))
    large = np.minimum(max_exact + (ratio * (NUM_BUCKETS - max_exact)).astype(np.int32), NUM_BUCKETS - 1)
    return np.where(n < max_exact, n, large).astype(np.int32)


def _bias_tile_kernel(tab_ref, bucket_ref, mask_ref, o_ref):
    c = pl.program_id(0)
    b = bucket_ref[...]
    acc = jnp.zeros(b.shape, F32)
    for k in range(NUM_BUCKETS):
        acc = jnp.where(b == k, tab_ref[k, c], acc)
    o_ref[0] = jnp.where(mask_ref[...] > 0, acc, NEG)


def bias_tiles(table, t, max_dist):
    ncol = table.shape[1]
    dist = np.arange(t)[:, None] + t - np.arange(2 * t)[None, :]
    mask = (dist >= 0) if max_dist is None else ((dist >= 0) & (dist <= max_dist))
    return pl.pallas_call(
        _bias_tile_kernel,
        grid=(ncol,),
        in_specs=[pl.BlockSpec(memory_space=pltpu.SMEM),
                  pl.BlockSpec((t, 2 * t), lambda c: (0, 0)),
                  pl.BlockSpec((t, 2 * t), lambda c: (0, 0))],
        out_specs=pl.BlockSpec((1, t, 2 * t), lambda c: (c, 0, 0)),
        out_shape=jax.ShapeDtypeStruct((ncol, t, 2 * t), F32),
        compiler_params=_cparams(("parallel",), 32),
        name="bias_tiles",
    )(table, jnp.asarray(_bucket_np(dist)), jnp.asarray(mask.astype(np.int32)))


def _diff_lambda(lq_ref, lk_ref, lam_init):
    e = jnp.exp(jnp.sum(lq_ref[...] * lk_ref[...], axis=-1, keepdims=True))
    return e[0:1] - e[1:2] + lam_init


def _split_maps(q):
    lane = lax.broadcasted_iota(jnp.int32, q.shape, 1)
    return jnp.where(lane < HALF, q, 0.0), jnp.where(lane >= HALF, q, 0.0)


def _attn_a_prompt_kernel(tab_ref, lq_ref, lk_ref, q_ref, k_ref, v_ref, bias_ref, sg_ref, o_ref,
                          k_scr, v_scr, s_scr, m_scr, l_scr, acc_scr, *, lam_init, scale):
    kvh = pl.program_id(0)
    qi = pl.program_id(2)
    tq = q_ref.shape[0]
    nt = k_scr.shape[0]
    ng = GROUP_A
    nrg = 2 * ng

    @pl.when(qi == 0)
    def _():
        for t in range(nt):
            k_scr[t] = k_ref[t * tq:(t + 1) * tq, :].astype(BF16)
            v_scr[t] = v_ref[t * tq:(t + 1) * tq, :].astype(BF16)

    q = q_ref[...] * scale
    parts1, parts2 = [], []
    for g in range(ng):
        q1, q2 = _split_maps(q[:, g * LANES:(g + 1) * LANES])
        parts1.append(q1)
        parts2.append(q2)
    qs = jnp.concatenate(parts1 + parts2, axis=0).astype(BF16)

    m_scr[...] = jnp.full(m_scr.shape, -jnp.inf, F32)

    def score_tile(t, bias_of_group):
        s = _dot_nt(qs, k_scr[t])
        for rg in range(nrg):
            rows = slice(rg * tq, (rg + 1) * tq)
            sg = s[rows] + bias_of_group(rg)
            s_scr[t, rows, :] = sg
            mx = sg[:, :LANES]
            for c in range(1, tq // LANES):
                mx = jnp.maximum(mx, sg[:, c * LANES:(c + 1) * LANES])
            m_scr[rows, :] = jnp.maximum(m_scr[rows, :], mx)

    def col_of_group(rg):
        m, g = divmod(rg, ng)
        return 2 * g + m

    def far_body(t, carry):
        score_tile(t, lambda rg: tab_ref[NUM_BUCKETS - 1, kvh * nrg + col_of_group(rg)])
        return carry

    lax.fori_loop(0, qi - 1, far_body, 0)

    @pl.when(qi >= 1)
    def _():
        score_tile(qi - 1, lambda rg: bias_ref[col_of_group(rg), :, 0:tq])

    score_tile(qi, lambda rg: bias_ref[col_of_group(rg), :, tq:2 * tq])

    m_row = jnp.max(m_scr[...], axis=-1, keepdims=True)
    l_scr[...] = jnp.zeros(l_scr.shape, F32)
    acc_scr[...] = jnp.zeros(acc_scr.shape, F32)

    def pv_body(t, carry):
        p = jnp.exp(s_scr[t] - m_row)
        ps = p[:, :LANES]
        for c in range(1, tq // LANES):
            ps = ps + p[:, c * LANES:(c + 1) * LANES]
        l_scr[...] += ps
        acc_scr[...] += _dot(p.astype(BF16), v_scr[t])
        return carry

    lax.fori_loop(0, qi + 1, pv_body, 0)

    l_row = jnp.sum(l_scr[...], axis=-1, keepdims=True)
    o_all = acc_scr[...] / l_row
    half = ng * tq
    lam = _diff_lambda(lq_ref, lk_ref, lam_init)
    o = o_all[:half] - lam * o_all[half:]
    for g in range(ng):
        og = _rms(o[g * tq:(g + 1) * tq], sg_ref[...]) * (1.0 - lam_init)
        o_ref[:, g * LANES:(g + 1) * LANES] = og.astype(o_ref.dtype)


def attn_a_prompt(qkv, nb, seq, table, bias_near, lam_q, lam_k, subln_g, lam_init):
    tq = TQ_A
    nq = seq // tq
    nqb = (N_KV_A * GROUP_A * LANES) // LANES
    kern = functools.partial(_attn_a_prompt_kernel, lam_init=lam_init, scale=HEAD_DIM_A ** -0.5)
    nrow = 2 * GROUP_A * tq
    return pl.pallas_call(
        kern,
        grid=(N_KV_A, nb, nq),
        in_specs=[pl.BlockSpec(memory_space=pltpu.SMEM),
                  pl.BlockSpec((2, HEAD_DIM_A), lambda h, b, i: (0, 0)),
                  pl.BlockSpec((2, HEAD_DIM_A), lambda h, b, i: (0, 0)),
                  pl.BlockSpec((tq, GROUP_A * LANES), lambda h, b, i: (b * nq + i, h)),
                  pl.BlockSpec((seq, LANES), lambda h, b, i: (b, nqb + h)),
                  pl.BlockSpec((seq, LANES), lambda h, b, i: (b, nqb + N_KV_A + h)),
                  pl.BlockSpec((2 * GROUP_A, tq, 2 * tq), lambda h, b, i: (h, 0, 0)),
                  pl.BlockSpec((1, LANES), lambda h, b, i: (0, 0))],
        out_specs=pl.BlockSpec((tq, GROUP_A * LANES), lambda h, b, i: (b * nq + i, h)),
        out_shape=jax.ShapeDtypeStruct((nb * seq, N_KV_A * GROUP_A * LANES), BF16),
        scratch_shapes=[pltpu.VMEM((nq, tq, LANES), BF16),
                        pltpu.VMEM((nq, tq, LANES), BF16),
                        pltpu.VMEM((nq, nrow, tq), F32),
                        pltpu.VMEM((nrow, LANES), F32),
                        pltpu.VMEM((nrow, LANES), F32),
                        pltpu.VMEM((nrow, LANES), F32)],
        compiler_params=_cparams(("parallel", "parallel", "arbitrary"), 56),
        name="attn_a_prompt",
    )(table, lam_q, lam_k, qkv, qkv, qkv, bias_near, subln_g.reshape(1, LANES))


def _attn_a_sample_kernel(pt_ref, lq_ref, lk_ref, q_ref, *rest, lam_init, scale, npp):
    page_refs = rest[:npp]
    (knew_ref, bfar_ref, blast_ref, bnew_ref, sg_ref, o_ref, m_scr, l_scr, acc_scr) = rest[npp:]
    ps = pl.program_id(1)
    nsteps = pl.num_programs(1)
    nkv = N_KV_A
    tdec = knew_ref.shape[1]

    @pl.when(ps == 0)
    def _():
        m_scr[...] = jnp.full(m_scr.shape, -jnp.inf, F32)
        l_scr[...] = jnp.zeros(l_scr.shape, F32)
        acc_scr[...] = jnp.zeros(acc_scr.shape, F32)

    qs32 = []
    for h in range(nkv):
        q1, q2 = _split_maps(q_ref[0, h] * scale)
        qs32.append(jnp.concatenate([q1, q2], axis=0))
    qs = [x.astype(BF16) for x in qs32]
    is_last = ps == nsteps - 1

    def update(h, s, pv):
        m_prev = m_scr[h]
        m_new = jnp.maximum(m_prev, jnp.max(s, axis=-1, keepdims=True))
        alpha = jnp.exp(m_prev - m_new)
        p = jnp.exp(s - m_new)
        l_scr[h] = alpha * l_scr[h] + jnp.sum(p, axis=-1, keepdims=True)
        acc_scr[h] = alpha * acc_scr[h] + pv(p)
        m_scr[h] = m_new

    for i in range(npp):
        tile = page_refs[i]
        for h in range(nkv):
            kt = tile[:, h * LANES:(h + 1) * LANES].astype(BF16)
            vt = tile[:, (nkv + h) * LANES:(nkv + h + 1) * LANES].astype(BF16)
            bias = bfar_ref[h]
            if i == npp - 1:
                bias = jnp.where(is_last, blast_ref[h], bias)
            s = _dot_nt(qs[h], kt) + bias
            update(h, s, lambda p, vt=vt: _dot(p.astype(BF16), vt))

    @pl.when(is_last)
    def _():
        lam = _diff_lambda(lq_ref, lk_ref, lam_init)
        knew = knew_ref[0]
        for h in range(nkv):
            kn = knew[:, h * LANES:(h + 1) * LANES]
            vn = knew[:, (nkv + h) * LANES:(nkv + h + 1) * LANES]
            bnew = bnew_ref[h]
            cols = [jnp.sum(qs32[h] * kn[t:t + 1, :], axis=-1, keepdims=True) + bnew[:, t:t + 1]
                    for t in range(tdec)]
            m_prev = m_scr[h]
            m_new = m_prev
            for c in cols:
                m_new = jnp.maximum(m_new, c)
            alpha = jnp.exp(m_prev - m_new)
            l = alpha * l_scr[h]
            acc = alpha * acc_scr[h]
            for t in range(tdec):
                p = jnp.exp(cols[t] - m_new)
                l = l + p
                acc = acc + p * vn[t:t + 1, :]
            o_all = acc / l
            half = o_all.shape[0] // 2
            o = o_all[:half] - lam * o_all[half:]
            o_ref[0, h] = (_rms(o, sg_ref[...]) * (1.0 - lam_init)).astype(o_ref.dtype)


def attn_a_sample(q_r, cache, layer, page_table, knew, bias_far, bias_last, bias_new,
                  lam_q, lam_k, subln_g, lam_init):
    nb, nkv, ngt, _ = q_r.shape
    n_pages = page_table.shape[1]
    page = cache.shape[2]
    width = cache.shape[3]
    npp = PAGES_PER_STEP
    assert n_pages % npp == 0
    tdec = knew.shape[1]
    kern = functools.partial(_attn_a_sample_kernel, lam_init=lam_init, scale=HEAD_DIM_A ** -0.5, npp=npp)

    def page_spec(i):
        return pl.BlockSpec((None, None, page, width),
                            lambda b, p, pt: (layer, pt[b, p * npp + i], 0, 0))

    const3 = lambda b, p, pt: (0, 0, 0)
    grid_spec = pltpu.PrefetchScalarGridSpec(
        num_scalar_prefetch=1,
        grid=(nb, n_pages // npp),
        in_specs=[pl.BlockSpec((2, HEAD_DIM_A), lambda b, p, pt: (0, 0)),
                  pl.BlockSpec((2, HEAD_DIM_A), lambda b, p, pt: (0, 0)),
                  pl.BlockSpec((1, nkv, ngt, LANES), lambda b, p, pt: (b, 0, 0, 0))]
                 + [page_spec(i) for i in range(npp)]
                 + [pl.BlockSpec((1, tdec, width), lambda b, p, pt: (b, 0, 0)),
                    pl.BlockSpec((nkv, 2 * ngt, LANES), const3),
                    pl.BlockSpec((nkv, 2 * ngt, LANES), const3),
                    pl.BlockSpec((nkv, 2 * ngt, LANES), const3),
                    pl.BlockSpec((1, LANES), lambda b, p, pt: (0, 0))],
        out_specs=pl.BlockSpec((1, nkv, ngt, LANES), lambda b, p, pt: (b, 0, 0, 0)),
        scratch_shapes=[pltpu.VMEM((nkv, 2 * ngt, LANES), F32),
                        pltpu.VMEM((nkv, 2 * ngt, LANES), F32),
                        pltpu.VMEM((nkv, 2 * ngt, LANES), F32)],
    )
    return pl.pallas_call(
        kern,
        grid_spec=grid_spec,
        out_shape=jax.ShapeDtypeStruct((nb, nkv, ngt, LANES), BF16),
        compiler_params=_cparams(("parallel", "arbitrary"), 48),
        name="attn_a_sample",
    )(page_table, lam_q, lam_k, q_r, *([cache] * npp), knew, bias_far, bias_last, bias_new,
      subln_g.reshape(1, LANES))


def _dup_half(x, odd):
    lane = lax.broadcasted_iota(jnp.int32, x.shape, 1)
    rolled = pltpu.roll(x, HALF, axis=1)
    keep = (lane >= HALF) if odd else (lane < HALF)
    return jnp.where(keep, x, rolled)


def _swa_prompt_kernel(sink_ref, q_ref, kvp_ref, kvo_ref, bias_ref, o_ref, *, scale):
    n = pl.program_id(1)
    w = q_ref.shape[0]
    ng = GROUP_C
    kv = jnp.concatenate([kvp_ref[...], kvo_ref[...]], axis=0)
    q = q_ref[...] * scale
    lane = lax.broadcasted_iota(jnp.int32, (w, LANES), 1)
    kcol = lax.broadcasted_iota(jnp.int32, (ng * w, 2 * w), 1)
    first = n == 0
    vpair0 = (N_KV_C * HEAD_DIM_C) // LANES
    for h in range(N_KV_C):
        kd = _dup_half(kv[:, (h // 2) * LANES:(h // 2 + 1) * LANES], h % 2).astype(BF16)
        vd = _dup_half(kv[:, (vpair0 + h // 2) * LANES:(vpair0 + h // 2 + 1) * LANES], h % 2).astype(BF16)
        parts, sinks = [], []
        for g in range(ng):
            qp = q[:, (h * ng // 2 + g // 2) * LANES:(h * ng // 2 + g // 2 + 1) * LANES]
            keep = (lane >= HALF) if g % 2 else (lane < HALF)
            parts.append(jnp.where(keep, qp, 0.0))
            sinks.append(jnp.full((w, 1), sink_ref[h * ng + g], F32))
        qs = jnp.concatenate(parts, axis=0).astype(BF16)
        sink = jnp.concatenate(sinks, axis=0)
        bias = bias_ref[h * ng:(h + 1) * ng].reshape(ng * w, 2 * w)
        bias = jnp.where(jnp.logical_and(first, kcol < w), NEG, bias)
        s = _dot_nt(qs, kd) + bias
        m = jnp.maximum(jnp.max(s, axis=-1, keepdims=True), sink)
        p = jnp.exp(s - m)
        den = jnp.sum(p, axis=-1, keepdims=True) + jnp.exp(sink - m)
        o = _dot(p.astype(BF16), vd) / den
        for j in range(ng // 2):
            oe = o[(2 * j) * w:(2 * j + 1) * w]
            oo = o[(2 * j + 1) * w:(2 * j + 2) * w]
            col = (h * ng // 2 + j) * LANES
            o_ref[:, col:col + LANES] = jnp.where(lane < HALF, oe, oo).astype(o_ref.dtype)


def swa_prompt(qkv, nb, seq, sinks, bias_swa):
    w = WINDOW
    nblk = seq // w
    dq = N_KV_C * GROUP_C * HEAD_DIM_C
    dkv = 2 * N_KV_C * HEAD_DIM_C
    kern = functools.partial(_swa_prompt_kernel, scale=HEAD_DIM_C ** -0.5)
    return pl.pallas_call(
        kern,
        grid=(nb, nblk),
        in_specs=[pl.BlockSpec(memory_space=pltpu.SMEM),
                  pl.BlockSpec((w, dq), lambda b, n: (b * nblk + n, 0)),
                  pl.BlockSpec((w, dkv), lambda b, n: (b * nblk + jnp.maximum(n - 1, 0), dq // dkv)),
                  pl.BlockSpec((w, dkv), lambda b, n: (b * nblk + n, dq // dkv)),
                  pl.BlockSpec((N_KV_C * GROUP_C, w, 2 * w), lambda b, n: (0, 0, 0))],
        out_specs=pl.BlockSpec((w, dq), lambda b, n: (b * nblk + n, 0)),
        out_shape=jax.ShapeDtypeStruct((nb * seq, dq), BF16),
        compiler_params=_cparams(("parallel", "parallel"), 48),
        name="swa_prompt",
    )(sinks, qkv, qkv, qkv, bias_swa)


def _swa_sample_kernel(q_ref, k_ref, v_ref, bias_ref, sink_ref, o_ref, *, scale):
    for h in range(N_KV_C):
        q = (q_ref[0, h] * scale).astype(BF16)
        s = _dot_nt(q, k_ref[0, h].astype(BF16)) + bias_ref[h]
        sink = sink_ref[h][:, 0:1]
        m = jnp.maximum(jnp.max(s, axis=-1, keepdims=True), sink)
        p = jnp.exp(s - m)
        den = jnp.sum(p, axis=-1, keepdims=True) + jnp.exp(sink - m)
        o_ref[0, h] = (_dot(p.astype(BF16), v_ref[0, h].astype(BF16)) / den).astype(o_ref.dtype)


def swa_sample(q_r, k_r, v_r, bias, sink_rows):
    nb, nkv, ngt, d = q_r.shape
    keys = k_r.shape[2]
    kern = functools.partial(_swa_sample_kernel, scale=HEAD_DIM_C ** -0.5)
    return pl.pallas_call(
        kern,
        grid=(nb,),
        in_specs=[pl.BlockSpec((1, nkv, ngt, d), lambda b: (b, 0, 0, 0)),
                  pl.BlockSpec((1, nkv, keys, d), lambda b: (b, 0, 0, 0)),
                  pl.BlockSpec((1, nkv, keys, d), lambda b: (b, 0, 0, 0)),
                  pl.BlockSpec((nkv, ngt, keys), lambda b: (0, 0, 0)),
                  pl.BlockSpec((nkv, ngt, LANES), lambda b: (0, 0, 0))],
        out_specs=pl.BlockSpec((1, nkv, ngt, d), lambda b: (b, 0, 0, 0)),
        out_shape=jax.ShapeDtypeStruct((nb, nkv, ngt, d), BF16),
        compiler_params=_cparams(("parallel",), 32),
        name="swa_sample",
    )(q_r, k_r, v_r, bias, sink_rows)


def _lower_bound(lbp_ref, layer):
    x = lbp_ref[...]
    e = jnp.exp(x - jnp.max(x, axis=0, keepdims=True))
    den = jnp.sum(e, axis=0, keepdims=True)
    num = e[1:2]
    for l in range(2, layer + 1):
        num = num + e[l:l + 1]
    if layer == 0:
        num = jnp.zeros_like(den)
    return num / den


def _cumsum_rows(x):
    rows = x.shape[0]
    row = lax.broadcasted_iota(jnp.int32, x.shape, 0)
    sh = 1
    while sh < rows:
        x = x + jnp.where(row >= sh, pltpu.roll(x, sh, axis=0), 0.0)
        sh *= 2
    return x


def _gla_gates(qr, fr, lb):
    q = qr * _sigmoid(qr)
    f = lb + (1.0 - lb) * _sigmoid(fr)
    return q, 1.0 - f, jnp.log(f)


def _gla_scores(q, k, bcum, sub):
    rows = q.shape[0]
    row = lax.broadcasted_iota(jnp.int32, (rows, LANES), 0)
    srow = lax.broadcasted_iota(jnp.int32, (sub, LANES), 0)
    lane = lax.broadcasted_iota(jnp.int32, (sub, LANES), 1)
    out = []
    for i in range(rows // sub):
        lo = i * sub
        qi, ki, bi = q[lo:lo + sub], k[lo:lo + sub], bcum[lo:lo + sub]
        a = jnp.zeros((sub, LANES), F32)
        if i > 0:
            ref = bcum[lo - 1:lo]
            qd = qi * jnp.exp(bi - ref)
            kd = k * jnp.exp(jnp.where(row < lo, ref - bcum, -jnp.inf))
            a = _dot_nt(qd.astype(BF16), kd.astype(BF16))
            if rows < LANES:
                a = jnp.concatenate([a, jnp.zeros((sub, LANES - rows), F32)], axis=1)
        for s in range(sub):
            d = jnp.where(srow >= s, bi - bi[s:s + 1], -jnp.inf)
            col = jnp.sum(qi * jnp.exp(d) * ki[s:s + 1], axis=-1, keepdims=True)
            a = jnp.where(lane == lo + s, col, a)
        out.append(a)
    return out[0] if len(out) == 1 else jnp.concatenate(out, axis=0)


def _pad_rows(x, rows):
    if x.shape[0] == rows:
        return x
    return jnp.concatenate([x, jnp.zeros((rows - x.shape[0], x.shape[1]), x.dtype)], axis=0)


def _gla_chunk(st, q, k, v, g, sub, last):
    bcum = _cumsum_rows(g)
    b_last = bcum[last:last + 1]
    o = _dot_nt((q * jnp.exp(bcum)).astype(BF16), st.astype(BF16))
    a = _gla_scores(q, k, bcum, sub)
    v128 = _pad_rows(v, LANES).astype(BF16)
    o = o + _dot(a.astype(BF16), v128)
    kdec = _pad_rows(k * jnp.exp(b_last - bcum), LANES).astype(BF16)
    vt = _pad_rows(v, LANES).T.astype(BF16)
    st_new = st * jnp.exp(b_last) + _dot(vt, kdec)
    return o, st_new


def _gla_finish(o, gate, ng_ref):
    return (_rms(o, ng_ref[...]) * (gate * _sigmoid(gate))).astype(BF16)


def _hgrn_prompt_kernel(lbp_ref, hq_ref, hf_ref, hi_ref, hg_ref, ng_ref, y_ref, sfin_ref, st_scr, *, layer):
    r = pl.program_id(2)
    c = CHUNK_B

    @pl.when(r == 0)
    def _():
        st_scr[...] = jnp.zeros(st_scr.shape, F32)

    lb = _lower_bound(lbp_ref, layer)

    def body(ci, carry):
        r0 = pl.multiple_of(ci * c, c)
        rows = pl.ds(r0, c)
        q, k, g = _gla_gates(hq_ref[rows, :], hf_ref[rows, :], lb)
        o, st_new = _gla_chunk(st_scr[...], q, k, hi_ref[rows, :], g, SUB_B, c - 1)
        st_scr[...] = st_new
        y_ref[rows, :] = _gla_finish(o, hg_ref[rows, :], ng_ref)
        return carry

    lax.fori_loop(0, hq_ref.shape[0] // c, body, 0)

    @pl.when(r == pl.num_programs(2) - 1)
    def _():
        sfin_ref[0, 0] = st_scr[...].T


def hgrn_prompt(h_in, nb, seq, lb_param, norm_g, layer):
    nh = h_in.shape[1] // (4 * LANES)
    rows = min(ROWS_B, seq)
    nr = seq // rows
    depth = lb_param.shape[0]
    kern = functools.partial(_hgrn_prompt_kernel, layer=layer)

    def col(j):
        return pl.BlockSpec((rows, LANES), lambda b, h, r: (b * nr + r, j * nh + h))

    return pl.pallas_call(
        kern,
        grid=(nb, nh, nr),
        in_specs=[pl.BlockSpec((depth, LANES), lambda b, h, r: (0, h)),
                  col(0), col(1), col(2), col(3),
                  pl.BlockSpec((1, LANES), lambda b, h, r: (0, 0))],
        out_specs=[pl.BlockSpec((rows, LANES), lambda b, h, r: (b * nr + r, h)),
                   pl.BlockSpec((1, 1, LANES, LANES), lambda b, h, r: (b, h, 0, 0))],
        out_shape=[jax.ShapeDtypeStruct((nb * seq, nh * LANES), BF16),
                   jax.ShapeDtypeStruct((nb, nh, LANES, LANES), F32)],
        scratch_shapes=[pltpu.VMEM((LANES, LANES), F32)],
        compiler_params=_cparams(("parallel", "parallel", "arbitrary"), 32),
        name="hgrn_prompt",
    )(lb_param, h_in, h_in, h_in, h_in, norm_g.reshape(1, LANES))


def _hgrn_sample_kernel(lbp_ref, hq_ref, hf_ref, hi_ref, hg_ref, ng_ref, s0_ref, y_ref, s1_ref, *, layer, tdec):
    lb = _lower_bound(lbp_ref, layer)
    q, k, g = _gla_gates(hq_ref[0], hf_ref[0], lb)
    rows = q.shape[0]
    o, st_new = _gla_chunk(s0_ref[0, 0].T, q, k, hi_ref[0], g, rows, tdec - 1)
    y_ref[0] = _gla_finish(o, hg_ref[0], ng_ref)
    s1_ref[0, 0] = st_new.T


def hgrn_sample(h_in, state, state_layer, lb_param, norm_g, layer, tdec):
    nb, rows, width = h_in.shape
    nh = width // (4 * LANES)
    depth = lb_param.shape[0]
    kern = functools.partial(_hgrn_sample_kernel, layer=layer, tdec=tdec)

    def col(j):
        return pl.BlockSpec((1, rows, LANES), lambda b, h: (b, 0, j * nh + h))

    return pl.pallas_call(
        kern,
        grid=(nb, nh),
        in_specs=[pl.BlockSpec((depth, LANES), lambda b, h: (0, h)),
                  col(0), col(1), col(2), col(3),
                  pl.BlockSpec((1, LANES), lambda b, h: (0, 0)),
                  pl.BlockSpec((None, 1, 1, LANES, LANES), lambda b, h: (state_layer, b, h, 0, 0))],
        out_specs=[pl.BlockSpec((1, rows, LANES), lambda b, h: (b, 0, h)),
                   pl.BlockSpec((1, 1, LANES, LANES), lambda b, h: (b, h, 0, 0))],
        out_shape=[jax.ShapeDtypeStruct((nb, rows, nh * LANES), BF16),
                   jax.ShapeDtypeStruct((nb, nh, LANES, LANES), F32)],
        compiler_params=_cparams(("parallel", "parallel"), 32),
        name="hgrn_sample",
    )(lb_param, h_in, h_in, h_in, h_in, norm_g.reshape(1, LANES), state)


def _row_tile(m):
    for t in (640, 512, 256, 128, 64, 32, 16, 8):
        if m % t == 0:
            return t
    raise ValueError(f"unsupported row count {m}")


def _col_tile(n, cands):
    for t in cands:
        if n % t == 0:
            return t
    raise ValueError(f"unsupported column count {n}")


def kernel(x_prompt, x_sample, cache_kv_a, state_hgrn_b, cache_win_c, page_table, norm_g, a_w_in, a_w_out, a_lambda, a_subln_g, b_w_in, b_w_out, b_lower_bound, b_norm_g, c_w_in, c_w_out, c_sinks, rel_bias_table, ffn_w_gate_up, ffn_w_down):
    nb, seq, d = x_prompt.shape
    nbs, tdec, _ = x_sample.shape
    depth = norm_g.shape[0]
    mp, ms = nb * seq, nbs * tdec
    m = mp + ms
    tm = _row_tile(m)
    page = cache_kv_a.shape[2]
    kvw = N_KV_A * 2 * HEAD_DIM_A
    cache = cache_kv_a.reshape(cache_kv_a.shape[0], cache_kv_a.shape[1], page, 2 * kvw)
    table = rel_bias_table.astype(F32)

    bias_a = bias_tiles(table, TQ_A, None)
    bias_c = bias_tiles(table, WINDOW, WINDOW)

    ga, gc = GROUP_A, GROUP_C
    col_a = np.array([[8 * h + 2 * g + mm for mm in range(2) for g in range(ga) for _ in range(tdec)]
                      for h in range(N_KV_A)])
    t_a = np.array([t for _ in range(2) for _ in range(ga) for t in range(tdec)])
    far_a = jnp.broadcast_to(table[NUM_BUCKETS - 1][col_a][:, :, None], col_a.shape + (LANES,))
    last_a = bias_a[col_a, t_a[None, :], TQ_A - page:TQ_A]
    new_a = bias_a[col_a, t_a[None, :], TQ_A:TQ_A + LANES]
    col_c = np.array([[gc * h + g for g in range(gc) for _ in range(tdec)] for h in range(N_KV_C)])
    t_c = np.array([t for _ in range(gc) for t in range(tdec)])
    wb = cache_win_c.shape[2]
    keys_c = 2 * WINDOW
    bias_cs = jnp.concatenate([bias_c[col_c, t_c[None, :], WINDOW - wb:WINDOW + tdec],
                               jnp.full(col_c.shape + (keys_c - wb - tdec,), NEG, F32)], axis=-1)

    h = jnp.concatenate([x_prompt.reshape(mp, d), x_sample.reshape(ms, d)], axis=0)
    kv_p, kv_s, hg_p, hg_s, win_p, win_s = [], [], [], [], [], []
    for layer in range(depth):
        kind, j = layer % 3, layer // 3
        g = norm_g[layer]
        if kind == 0:
            lam_init = 0.8 - 0.6 * math.exp(-0.3 * layer)
            w_in = a_w_in[j].astype(BF16)
            qkv = norm_matmul(h, g[0], w_in, tm=tm, tn=_col_tile(w_in.shape[1], (1024, 512, 256, 128)))
            dq = N_KV_A * GROUP_A * 2 * HEAD_DIM_A
            lam_q, lam_k = a_lambda[j][0::2], a_lambda[j][1::2]
            yp = attn_a_prompt(qkv, nb, seq, table, bias_a, lam_q, lam_k, a_subln_g[j], lam_init)
            qs = qkv[mp:, :dq].reshape(nbs, tdec, N_KV_A, ga, LANES).transpose(0, 2, 3, 1, 4)
            qs = qs.reshape(nbs, N_KV_A, ga * tdec, LANES)
            kvs = qkv[mp:, dq:].reshape(nbs, tdec, 2 * kvw)
            ys = attn_a_sample(qs, cache, j, page_table, kvs, far_a, last_a, new_a,
                               lam_q, lam_k, a_subln_g[j], lam_init)
            ys = ys.reshape(nbs, N_KV_A, ga, tdec, LANES).transpose(0, 3, 1, 2, 4).reshape(ms, dq)
            kv_p.append(qkv[:mp, dq:].reshape(nb, seq, 2, N_KV_A, 2 * HEAD_DIM_A))
            kv_s.append(kvs.reshape(nbs, tdec, 2, N_KV_A, 2 * HEAD_DIM_A))
            w_out = a_w_out[j]
        elif kind == 1:
            w_in = b_w_in[j].astype(BF16)
            hin = norm_matmul(h, g[0], w_in, tm=tm, tn=_col_tile(w_in.shape[1], (1024, 512, 256, 128)))
            yp, sp = hgrn_prompt(hin, nb, seq, b_lower_bound, b_norm_g[j], layer)
            hs_in = jnp.pad(hin[mp:].reshape(nbs, tdec, hin.shape[1]), ((0, 0), (0, 8 - tdec), (0, 0)))
            ys, ss = hgrn_sample(hs_in, state_hgrn_b, j, b_lower_bound, b_norm_g[j], layer, tdec)
            ys = ys[:, :tdec].reshape(ms, d)
            hg_p.append(sp)
            hg_s.append(ss)
            w_out = b_w_out[j]
        else:
            w_in = c_w_in[j].astype(BF16)
            qkv = norm_matmul(h, g[0], w_in, tm=tm, tn=_col_tile(w_in.shape[1], (1280, 512, 256, 128)))
            dq = N_KV_C * GROUP_C * HEAD_DIM_C
            dkv = N_KV_C * HEAD_DIM_C
            yp = swa_prompt(qkv, nb, seq, c_sinks[j], bias_c)
            kv_new = qkv[mp:, dq:].reshape(nbs, tdec, 2 * dkv)
            kv_all = jnp.concatenate([cache_win_c[j].reshape(nbs, wb, 2 * dkv), kv_new], axis=1)
            kv_pad = jnp.pad(kv_all, ((0, 0), (0, keys_c - wb - tdec), (0, 0)))
            kv_r = kv_pad.reshape(nbs, keys_c, 2, N_KV_C, HEAD_DIM_C).transpose(2, 0, 3, 1, 4)
            qs = qkv[mp:, :dq].reshape(nbs, tdec, N_KV_C, gc, HEAD_DIM_C).transpose(0, 2, 3, 1, 4)
            qs = qs.reshape(nbs, N_KV_C, gc * tdec, HEAD_DIM_C)
            sink_rows = jnp.broadcast_to(c_sinks[j][col_c][:, :, None], col_c.shape + (LANES,))
            ys = swa_sample(qs, kv_r[0], kv_r[1], bias_cs, sink_rows)
            ys = ys.reshape(nbs, N_KV_C, gc, tdec, HEAD_DIM_C).transpose(0, 3, 1, 2, 4).reshape(ms, dq)
            win_p.append(qkv[:mp, dq:].reshape(nb, seq, 2, N_KV_C, HEAD_DIM_C)[:, seq - wb:])
            win_s.append(kv_all[:, tdec:].reshape(nbs, wb, 2, N_KV_C, HEAD_DIM_C))
            w_out = c_w_out[j]
        y = jnp.concatenate([yp, ys], axis=0)
        h = matmul_postnorm_residual(y, w_out.astype(BF16), g[1], h, tm=tm, tk=512)
        w_gu = ffn_w_gate_up[layer].astype(BF16)
        dff = w_gu.shape[1] // 2
        hm = norm_swiglu(h, g[2], w_gu, tm=tm, tn=_col_tile(dff, (512, 256, 128)))
        h = matmul_postnorm_residual(hm, ffn_w_down[layer].astype(BF16), g[3], h, tm=tm, tk=512)
    return (h[:mp].reshape(nb, seq, d), h[mp:].reshape(nbs, tdec, d),
            jnp.stack(kv_p), jnp.stack(kv_s), jnp.stack(hg_p), jnp.stack(hg_s),
            jnp.stack(win_p), jnp.stack(win_s))
```

```python
import functools
import math

import numpy as np
import jax
import jax.numpy as jnp
from jax import lax
from jax.experimental import pallas as pl
from jax.experimental.pallas import tpu as pltpu

F32 = jnp.float32
BF16 = jnp.bfloat16

NORM_EPS = 1e-6
NUM_BUCKETS = 32
MAX_DISTANCE = 128
NEG = -1e30
LANES = 128
HALF = LANES // 2
MIB = 1024 * 1024

HEAD_DIM_A = 64
N_KV_A = 4
GROUP_A = 4
HEAD_DIM_C = 64
N_KV_C = 4
GROUP_C = 8
WINDOW = 128
TQ_A = 256
PAGES_PER_STEP = 16
CHUNK_B = 128
SUB_B = 16
ROWS_B = 512


def _cparams(sem, vmem_mib):
    return pltpu.CompilerParams(dimension_semantics=sem, vmem_limit_bytes=vmem_mib * MIB)


def _rms(x, g):
    ms = jnp.mean(x * x, axis=-1, keepdims=True)
    return x * lax.rsqrt(ms + NORM_EPS) * g


def _sigmoid(x):
    return 1.0 / (1.0 + jnp.exp(-x))


def _dot(a, b):
    return jnp.dot(a, b, preferred_element_type=F32)


def _dot_nt(a, b):
    return lax.dot_general(a, b, (((1,), (1,)), ((), ())), preferred_element_type=F32)


def _norm_matmul_kernel(x_ref, g_ref, w_ref, o_ref, xn_ref):
    @pl.when(pl.program_id(1) == 0)
    def _():
        xn_ref[...] = _rms(x_ref[...], g_ref[...]).astype(BF16)

    o_ref[...] = _dot(xn_ref[...], w_ref[...]).astype(o_ref.dtype)


def norm_matmul(x, g, w, *, tm, tn):
    m, k = x.shape
    n = w.shape[1]
    assert m % tm == 0 and n % tn == 0
    return pl.pallas_call(
        _norm_matmul_kernel,
        grid=(m // tm, n // tn),
        in_specs=[pl.BlockSpec((tm, k), lambda i, j: (i, 0)),
                  pl.BlockSpec((1, k), lambda i, j: (0, 0)),
                  pl.BlockSpec((k, tn), lambda i, j: (0, j))],
        out_specs=pl.BlockSpec((tm, tn), lambda i, j: (i, j)),
        out_shape=jax.ShapeDtypeStruct((m, n), F32),
        scratch_shapes=[pltpu.VMEM((tm, k), BF16)],
        compiler_params=_cparams(("parallel", "arbitrary"), 48),
        name="norm_matmul",
    )(x, g.reshape(1, k), w)


def _norm_swiglu_kernel(x_ref, g_ref, wg_ref, wu_ref, o_ref, xn_ref):
    @pl.when(pl.program_id(1) == 0)
    def _():
        xn_ref[...] = _rms(x_ref[...], g_ref[...]).astype(BF16)

    xn = xn_ref[...]
    gate = _dot(xn, wg_ref[...])
    up = _dot(xn, wu_ref[...])
    o_ref[...] = (gate * _sigmoid(gate) * up).astype(o_ref.dtype)


def norm_swiglu(x, g, w_gate_up, *, tm, tn):
    m, k = x.shape
    dff = w_gate_up.shape[1] // 2
    assert m % tm == 0 and dff % tn == 0
    nj = dff // tn
    return pl.pallas_call(
        _norm_swiglu_kernel,
        grid=(m // tm, nj),
        in_specs=[pl.BlockSpec((tm, k), lambda i, j: (i, 0)),
                  pl.BlockSpec((1, k), lambda i, j: (0, 0)),
                  pl.BlockSpec((k, tn), lambda i, j: (0, j)),
                  pl.BlockSpec((k, tn), lambda i, j: (0, j + nj))],
        out_specs=pl.BlockSpec((tm, tn), lambda i, j: (i, j)),
        out_shape=jax.ShapeDtypeStruct((m, dff), BF16),
        scratch_shapes=[pltpu.VMEM((tm, k), BF16)],
        compiler_params=_cparams(("parallel", "arbitrary"), 48),
        name="norm_swiglu",
    )(x, g.reshape(1, k), w_gate_up, w_gate_up)


def _matmul_postnorm_kernel(y_ref, w_ref, g_ref, h_ref, o_ref, acc_ref, *, nk):
    def finish(acc):
        o_ref[...] = h_ref[...] + _rms(acc, g_ref[...])

    if nk == 1:
        finish(_dot(y_ref[...], w_ref[...]))
        return
    k = pl.program_id(1)

    @pl.when(k == 0)
    def _():
        acc_ref[...] = _dot(y_ref[...], w_ref[...])

    @pl.when(jnp.logical_and(k > 0, k < nk - 1))
    def _():
        acc_ref[...] += _dot(y_ref[...], w_ref[...])

    @pl.when(k == nk - 1)
    def _():
        finish(acc_ref[...] + _dot(y_ref[...], w_ref[...]))


def matmul_postnorm_residual(y, w, g, h, *, tm):
    m, kdim = y.shape
    n = w.shape[1]
    tk = kdim if kdim <= 2048 else _col_tile(kdim, (1408, 1024, 512, 256, 128))
    assert m % tm == 0 and kdim % tk == 0
    nk = kdim // tk
    return pl.pallas_call(
        functools.partial(_matmul_postnorm_kernel, nk=nk),
        grid=(m // tm, nk),
        in_specs=[pl.BlockSpec((tm, tk), lambda i, k: (i, k)),
                  pl.BlockSpec((tk, n), lambda i, k: (k, 0)),
                  pl.BlockSpec((1, n), lambda i, k: (0, 0)),
                  pl.BlockSpec((tm, n), lambda i, k: (i, 0))],
        out_specs=pl.BlockSpec((tm, n), lambda i, k: (i, 0)),
        out_shape=jax.ShapeDtypeStruct((m, n), F32),
        scratch_shapes=[pltpu.VMEM((tm, n) if nk > 1 else (8, LANES), F32)],
        compiler_params=_cparams(("parallel", "arbitrary"), 56),
        name="matmul_postnorm_residual",
    )(y, w, g.reshape(1, n), h)


def _bucket_np(dist):
    n = np.maximum(dist, 0)
    max_exact = NUM_BUCKETS // 2
    ratio = np.log(np.maximum(n, 1).astype(np.float32) / np.float32(max_exact)) / np.float32(
        math.log(MAX_DISTANCE / max_exact))
    large = np.minimum(max_exact + (ratio * (NUM_BUCKETS - max_exact)).astype(np.int32), NUM_BUCKETS - 1)
    return np.where(n < max_exact, n, large).astype(np.int32)


def _bias_tile_kernel(tab_ref, bucket_ref, mask_ref, o_ref):
    c = pl.program_id(0)
    b = bucket_ref[...]
    acc = jnp.zeros(b.shape, F32)
    for k in range(NUM_BUCKETS):
        acc = jnp.where(b == k, tab_ref[k, c], acc)
    o_ref[0] = jnp.where(mask_ref[...] > 0, acc, NEG)


def bias_tiles(table, dist, valid):
    ncol = table.shape[1]
    r, c = dist.shape
    return pl.pallas_call(
        _bias_tile_kernel,
        grid=(ncol,),
        in_specs=[pl.BlockSpec(memory_space=pltpu.SMEM),
                  pl.BlockSpec((r, c), lambda i: (0, 0)),
                  pl.BlockSpec((r, c), lambda i: (0, 0))],
        out_specs=pl.BlockSpec((1, r, c), lambda i: (i, 0, 0)),
        out_shape=jax.ShapeDtypeStruct((ncol, r, c), F32),
        compiler_params=_cparams(("parallel",), 32),
        name="bias_tiles",
    )(table, jnp.asarray(_bucket_np(dist)), jnp.asarray(valid.astype(np.int32)))


def _band_dist(t):
    return np.arange(t)[:, None] + t - np.arange(2 * t)[None, :]


def _decode_dist(tdec, n_past, cols):
    t = np.arange(8)[:, None]
    k = np.arange(cols)[None, :]
    dist = np.where(k < n_past, t + n_past - k, t - (k - n_past))
    inside = (k < n_past + tdec) & (t < tdec)
    return dist, inside


def _diff_lambda(lq_ref, lk_ref, lam_init):
    e = jnp.exp(jnp.sum(lq_ref[...] * lk_ref[...], axis=-1, keepdims=True))
    return e[0:1] - e[1:2] + lam_init


def _split_maps(q):
    lane = lax.broadcasted_iota(jnp.int32, q.shape, 1)
    return jnp.where(lane < HALF, q, 0.0), jnp.where(lane >= HALF, q, 0.0)


def _attn_a_prompt_kernel(tab_ref, lq_ref, lk_ref, q_ref, k_ref, v_ref, bias_ref, sg_ref, o_ref,
                          k_scr, v_scr, s_scr, m_scr, l_scr, acc_scr, *, lam_init, scale):
    kvh = pl.program_id(0)
    qi = pl.program_id(2)
    tq = q_ref.shape[0]
    nt = k_scr.shape[0]
    ng = GROUP_A
    nrg = 2 * ng

    @pl.when(qi == 0)
    def _():
        for t in range(nt):
            k_scr[t] = k_ref[t * tq:(t + 1) * tq, :].astype(BF16)
            v_scr[t] = v_ref[t * tq:(t + 1) * tq, :].astype(BF16)

    q = q_ref[...] * scale
    parts1, parts2 = [], []
    for g in range(ng):
        q1, q2 = _split_maps(q[:, g * LANES:(g + 1) * LANES])
        parts1.append(q1)
        parts2.append(q2)
    qs = jnp.concatenate(parts1 + parts2, axis=0).astype(BF16)

    m_scr[...] = jnp.full(m_scr.shape, -jnp.inf, F32)

    def score_tile(t, bias_of_group):
        s = _dot_nt(qs, k_scr[t])
        for rg in range(nrg):
            rows = slice(rg * tq, (rg + 1) * tq)
            sg = s[rows] + bias_of_group(rg)
            s_scr[t, rows, :] = sg
            mx = sg[:, :LANES]
            for c in range(1, tq // LANES):
                mx = jnp.maximum(mx, sg[:, c * LANES:(c + 1) * LANES])
            m_scr[rows, :] = jnp.maximum(m_scr[rows, :], mx)

    def col_of_group(rg):
        m, g = divmod(rg, ng)
        return 2 * g + m

    def far_body(t, carry):
        score_tile(t, lambda rg: tab_ref[NUM_BUCKETS - 1, kvh * nrg + col_of_group(rg)])
        return carry

    lax.fori_loop(0, qi - 1, far_body, 0)

    @pl.when(qi >= 1)
    def _():
        score_tile(qi - 1, lambda rg: bias_ref[col_of_group(rg), :, 0:tq])

    score_tile(qi, lambda rg: bias_ref[col_of_group(rg), :, tq:2 * tq])

    m_row = jnp.max(m_scr[...], axis=-1, keepdims=True)
    l_scr[...] = jnp.zeros(l_scr.shape, F32)
    acc_scr[...] = jnp.zeros(acc_scr.shape, F32)

    def pv_body(t, carry):
        p = jnp.exp(s_scr[t] - m_row)
        ps = p[:, :LANES]
        for c in range(1, tq // LANES):
            ps = ps + p[:, c * LANES:(c + 1) * LANES]
        l_scr[...] += ps
        acc_scr[...] += _dot(p.astype(BF16), v_scr[t])
        return carry

    lax.fori_loop(0, qi + 1, pv_body, 0)

    l_row = jnp.sum(l_scr[...], axis=-1, keepdims=True)
    o_all = acc_scr[...] / l_row
    half = ng * tq
    lam = _diff_lambda(lq_ref, lk_ref, lam_init)
    o = o_all[:half] - lam * o_all[half:]
    for g in range(ng):
        og = _rms(o[g * tq:(g + 1) * tq], sg_ref[...]) * (1.0 - lam_init)
        o_ref[:, g * LANES:(g + 1) * LANES] = og.astype(o_ref.dtype)


def attn_a_prompt(qkv, nb, seq, table, bias_near, lam_q, lam_k, subln_g, lam_init):
    tq = TQ_A
    nq = seq // tq
    nqb = (N_KV_A * GROUP_A * LANES) // LANES
    kern = functools.partial(_attn_a_prompt_kernel, lam_init=lam_init, scale=HEAD_DIM_A ** -0.5)
    nrow = 2 * GROUP_A * tq
    return pl.pallas_call(
        kern,
        grid=(N_KV_A, nb, nq),
        in_specs=[pl.BlockSpec(memory_space=pltpu.SMEM),
                  pl.BlockSpec((2, HEAD_DIM_A), lambda h, b, i: (0, 0)),
                  pl.BlockSpec((2, HEAD_DIM_A), lambda h, b, i: (0, 0)),
                  pl.BlockSpec((tq, GROUP_A * LANES), lambda h, b, i: (b * nq + i, h)),
                  pl.BlockSpec((seq, LANES), lambda h, b, i: (b, nqb + h)),
                  pl.BlockSpec((seq, LANES), lambda h, b, i: (b, nqb + N_KV_A + h)),
                  pl.BlockSpec((2 * GROUP_A, tq, 2 * tq), lambda h, b, i: (h, 0, 0)),
                  pl.BlockSpec((1, LANES), lambda h, b, i: (0, 0))],
        out_specs=pl.BlockSpec((tq, GROUP_A * LANES), lambda h, b, i: (b * nq + i, h)),
        out_shape=jax.ShapeDtypeStruct((nb * seq, N_KV_A * GROUP_A * LANES), BF16),
        scratch_shapes=[pltpu.VMEM((nq, tq, LANES), BF16),
                        pltpu.VMEM((nq, tq, LANES), BF16),
                        pltpu.VMEM((nq, nrow, tq), F32),
                        pltpu.VMEM((nrow, LANES), F32),
                        pltpu.VMEM((nrow, LANES), F32),
                        pltpu.VMEM((nrow, LANES), F32)],
        compiler_params=_cparams(("parallel", "parallel", "arbitrary"), 56),
        name="attn_a_prompt",
    )(table, lam_q, lam_k, qkv, qkv, qkv, bias_near, subln_g.reshape(1, LANES))


def _attn_a_sample_kernel(pt_ref, lq_ref, lk_ref, q_ref, *rest, lam_init, scale, npp):
    page_refs = rest[:npp]
    (knew_ref, bfar_ref, blast_ref, bnew_ref, sg_ref, o_ref, m_scr, l_scr, acc_scr) = rest[npp:]
    ps = pl.program_id(1)
    nsteps = pl.num_programs(1)
    nkv = N_KV_A
    tdec = knew_ref.shape[1]

    @pl.when(ps == 0)
    def _():
        m_scr[...] = jnp.full(m_scr.shape, -jnp.inf, F32)
        l_scr[...] = jnp.zeros(l_scr.shape, F32)
        acc_scr[...] = jnp.zeros(acc_scr.shape, F32)

    qs32 = []
    for h in range(nkv):
        q1, q2 = _split_maps(q_ref[0, h] * scale)
        qs32.append(jnp.concatenate([q1, q2], axis=0))
    qs = [x.astype(BF16) for x in qs32]
    is_last = ps == nsteps - 1

    nslot = 2 * nkv
    page = page_refs[0].shape[0] // nslot

    def slot_rows(slot):
        rows = [r[pl.ds(slot, page, stride=nslot), :] for r in page_refs]
        return jnp.concatenate(rows, axis=0).astype(BF16)

    for h in range(nkv):
        kh = slot_rows(h)
        vh = slot_rows(nkv + h)
        s = _dot_nt(qs[h], kh)
        bfar = bfar_ref[h]
        tail = jnp.where(is_last, blast_ref[h], bfar)
        s = jnp.concatenate([s[:, :-page] + bfar[:, 0:1], s[:, -page:] + tail], axis=1)
        m_prev = m_scr[h]
        m_new = jnp.maximum(m_prev, jnp.max(s, axis=-1, keepdims=True))
        alpha = jnp.exp(m_prev - m_new)
        p = jnp.exp(s - m_new[:, 0:1])
        l_scr[h] = alpha * l_scr[h] + jnp.sum(p, axis=-1, keepdims=True)
        acc_scr[h] = alpha * acc_scr[h] + _dot(p.astype(BF16), vh)
        m_scr[h] = m_new

    @pl.when(is_last)
    def _():
        lam = _diff_lambda(lq_ref, lk_ref, lam_init)
        knew = knew_ref[0]
        for h in range(nkv):
            kn = knew[:, h * LANES:(h + 1) * LANES]
            vn = knew[:, (nkv + h) * LANES:(nkv + h + 1) * LANES]
            bnew = bnew_ref[h]
            cols = [jnp.sum(qs32[h] * kn[t:t + 1, :], axis=-1, keepdims=True) + bnew[:, t:t + 1]
                    for t in range(tdec)]
            m_prev = m_scr[h]
            m_new = m_prev
            for c in cols:
                m_new = jnp.maximum(m_new, c)
            alpha = jnp.exp(m_prev - m_new)
            l = alpha * l_scr[h]
            acc = alpha * acc_scr[h]
            for t in range(tdec):
                p = jnp.exp(cols[t] - m_new)
                l = l + p
                acc = acc + p * vn[t:t + 1, :]
            o_all = acc / l
            half = o_all.shape[0] // 2
            o = o_all[:half] - lam * o_all[half:]
            o_ref[0, h] = (_rms(o, sg_ref[...]) * (1.0 - lam_init)).astype(o_ref.dtype)


def attn_a_sample(q_r, cache, layer, page_table, knew, bias_far, bias_last, bias_new,
                  lam_q, lam_k, subln_g, lam_init):
    nb, nkv, ngt, _ = q_r.shape
    n_pages = page_table.shape[1]
    page_rows = cache.shape[2]
    assert page_rows == 2 * nkv * LANES and cache.shape[3] == LANES
    npp = min(PAGES_PER_STEP, n_pages)
    assert n_pages % npp == 0
    tdec = knew.shape[1]
    width = knew.shape[2]
    kern = functools.partial(_attn_a_sample_kernel, lam_init=lam_init, scale=HEAD_DIM_A ** -0.5, npp=npp)

    def page_spec(i):
        return pl.BlockSpec((None, None, page_rows, LANES),
                            lambda b, p, pt: (layer, pt[b, p * npp + i], 0, 0))

    const3 = lambda b, p, pt: (0, 0, 0)
    grid_spec = pltpu.PrefetchScalarGridSpec(
        num_scalar_prefetch=1,
        grid=(nb, n_pages // npp),
        in_specs=[pl.BlockSpec((2, HEAD_DIM_A), lambda b, p, pt: (0, 0)),
                  pl.BlockSpec((2, HEAD_DIM_A), lambda b, p, pt: (0, 0)),
                  pl.BlockSpec((1, nkv, ngt, LANES), lambda b, p, pt: (b, 0, 0, 0))]
                 + [page_spec(i) for i in range(npp)]
                 + [pl.BlockSpec((1, tdec, width), lambda b, p, pt: (b, 0, 0)),
                    pl.BlockSpec((nkv, 2 * ngt, LANES), const3),
                    pl.BlockSpec((nkv, 2 * ngt, LANES), const3),
                    pl.BlockSpec((nkv, 2 * ngt, LANES), const3),
                    pl.BlockSpec((1, LANES), lambda b, p, pt: (0, 0))],
        out_specs=pl.BlockSpec((1, nkv, ngt, LANES), lambda b, p, pt: (b, 0, 0, 0)),
        scratch_shapes=[pltpu.VMEM((nkv, 2 * ngt, LANES), F32),
                        pltpu.VMEM((nkv, 2 * ngt, LANES), F32),
                        pltpu.VMEM((nkv, 2 * ngt, LANES), F32)],
    )
    return pl.pallas_call(
        kern,
        grid_spec=grid_spec,
        out_shape=jax.ShapeDtypeStruct((nb, nkv, ngt, LANES), BF16),
        compiler_params=_cparams(("parallel", "arbitrary"), 48),
        name="attn_a_sample",
    )(page_table, lam_q, lam_k, q_r, *([cache] * npp), knew, bias_far, bias_last, bias_new,
      subln_g.reshape(1, LANES))


def _dup_half(x, odd):
    lane = lax.broadcasted_iota(jnp.int32, x.shape, 1)
    rolled = pltpu.roll(x, HALF, axis=1)
    keep = (lane >= HALF) if odd else (lane < HALF)
    return jnp.where(keep, x, rolled)


def _swa_prompt_kernel(sink_ref, q_ref, kvp_ref, kvo_ref, bias_ref, o_ref, *, scale):
    n = pl.program_id(1)
    w = q_ref.shape[0]
    ng = GROUP_C
    kv = jnp.concatenate([kvp_ref[...], kvo_ref[...]], axis=0)
    q = q_ref[...] * scale
    lane = lax.broadcasted_iota(jnp.int32, (w, LANES), 1)
    kcol = lax.broadcasted_iota(jnp.int32, (ng * w, 2 * w), 1)
    first = n == 0
    vpair0 = (N_KV_C * HEAD_DIM_C) // LANES
    for h in range(N_KV_C):
        kd = _dup_half(kv[:, (h // 2) * LANES:(h // 2 + 1) * LANES], h % 2).astype(BF16)
        vd = _dup_half(kv[:, (vpair0 + h // 2) * LANES:(vpair0 + h // 2 + 1) * LANES], h % 2).astype(BF16)
        parts, sinks = [], []
        for g in range(ng):
            qp = q[:, (h * ng // 2 + g // 2) * LANES:(h * ng // 2 + g // 2 + 1) * LANES]
            keep = (lane >= HALF) if g % 2 else (lane < HALF)
            parts.append(jnp.where(keep, qp, 0.0))
            sinks.append(jnp.full((w, 1), sink_ref[h * ng + g], F32))
        qs = jnp.concatenate(parts, axis=0).astype(BF16)
        sink = jnp.concatenate(sinks, axis=0)
        bias = bias_ref[h * ng:(h + 1) * ng].reshape(ng * w, 2 * w)
        bias = jnp.where(jnp.logical_and(first, kcol < w), NEG, bias)
        s = _dot_nt(qs, kd) + bias
        m = jnp.maximum(jnp.max(s, axis=-1, keepdims=True), sink)
        p = jnp.exp(s - m)
        den = jnp.sum(p, axis=-1, keepdims=True) + jnp.exp(sink - m)
        o = _dot(p.astype(BF16), vd) / den
        for j in range(ng // 2):
            oe = o[(2 * j) * w:(2 * j + 1) * w]
            oo = o[(2 * j + 1) * w:(2 * j + 2) * w]
            col = (h * ng // 2 + j) * LANES
            o_ref[:, col:col + LANES] = jnp.where(lane < HALF, oe, oo).astype(o_ref.dtype)


def swa_prompt(qkv, nb, seq, sinks, bias_swa):
    w = WINDOW
    nblk = seq // w
    dq = N_KV_C * GROUP_C * HEAD_DIM_C
    dkv = 2 * N_KV_C * HEAD_DIM_C
    kern = functools.partial(_swa_prompt_kernel, scale=HEAD_DIM_C ** -0.5)
    return pl.pallas_call(
        kern,
        grid=(nb, nblk),
        in_specs=[pl.BlockSpec(memory_space=pltpu.SMEM),
                  pl.BlockSpec((w, dq), lambda b, n: (b * nblk + n, 0)),
                  pl.BlockSpec((w, dkv), lambda b, n: (b * nblk + jnp.maximum(n - 1, 0), dq // dkv)),
                  pl.BlockSpec((w, dkv), lambda b, n: (b * nblk + n, dq // dkv)),
                  pl.BlockSpec((N_KV_C * GROUP_C, w, 2 * w), lambda b, n: (0, 0, 0))],
        out_specs=pl.BlockSpec((w, dq), lambda b, n: (b * nblk + n, 0)),
        out_shape=jax.ShapeDtypeStruct((nb * seq, dq), BF16),
        compiler_params=_cparams(("parallel", "parallel"), 48),
        name="swa_prompt",
    )(sinks, qkv, qkv, qkv, bias_swa)


def _swa_sample_kernel(q_ref, k_ref, v_ref, bias_ref, sink_ref, o_ref, *, scale):
    for h in range(N_KV_C):
        q = (q_ref[0, h] * scale).astype(BF16)
        s = _dot_nt(q, k_ref[0, h].astype(BF16)) + bias_ref[h]
        sink = sink_ref[h][:, 0:1]
        m = jnp.maximum(jnp.max(s, axis=-1, keepdims=True), sink)
        p = jnp.exp(s - m)
        den = jnp.sum(p, axis=-1, keepdims=True) + jnp.exp(sink - m)
        o_ref[0, h] = (_dot(p.astype(BF16), v_ref[0, h].astype(BF16)) / den).astype(o_ref.dtype)


def swa_sample(q_r, k_r, v_r, bias, sink_rows):
    nb, nkv, ngt, d = q_r.shape
    keys = k_r.shape[2]
    kern = functools.partial(_swa_sample_kernel, scale=HEAD_DIM_C ** -0.5)
    return pl.pallas_call(
        kern,
        grid=(nb,),
        in_specs=[pl.BlockSpec((1, nkv, ngt, d), lambda b: (b, 0, 0, 0)),
                  pl.BlockSpec((1, nkv, keys, d), lambda b: (b, 0, 0, 0)),
                  pl.BlockSpec((1, nkv, keys, d), lambda b: (b, 0, 0, 0)),
                  pl.BlockSpec((nkv, ngt, keys), lambda b: (0, 0, 0)),
                  pl.BlockSpec((nkv, ngt, LANES), lambda b: (0, 0, 0))],
        out_specs=pl.BlockSpec((1, nkv, ngt, d), lambda b: (b, 0, 0, 0)),
        out_shape=jax.ShapeDtypeStruct((nb, nkv, ngt, d), BF16),
        compiler_params=_cparams(("parallel",), 32),
        name="swa_sample",
    )(q_r, k_r, v_r, bias, sink_rows)


def _lower_bound(lbp_ref, layer):
    x = lbp_ref[...]
    e = jnp.exp(x - jnp.max(x, axis=0, keepdims=True))
    den = jnp.sum(e, axis=0, keepdims=True)
    num = e[1:2]
    for l in range(2, layer + 1):
        num = num + e[l:l + 1]
    if layer == 0:
        num = jnp.zeros_like(den)
    return num / den


def _cumsum_rows(x):
    rows = x.shape[0]
    row = lax.broadcasted_iota(jnp.int32, x.shape, 0)
    sh = 1
    while sh < rows:
        x = x + jnp.where(row >= sh, pltpu.roll(x, sh, axis=0), 0.0)
        sh *= 2
    return x


def _gla_gates(qr, fr, lb):
    q = qr * _sigmoid(qr)
    f = lb + (1.0 - lb) * _sigmoid(fr)
    return q, 1.0 - f, jnp.log(f)


def _gla_scores(q, k, bcum, sub):
    rows = q.shape[0]
    row = lax.broadcasted_iota(jnp.int32, (rows, LANES), 0)
    srow = lax.broadcasted_iota(jnp.int32, (sub, LANES), 0)
    lane = lax.broadcasted_iota(jnp.int32, (sub, LANES), 1)
    out = []
    for i in range(rows // sub):
        lo = i * sub
        qi, ki, bi = q[lo:lo + sub], k[lo:lo + sub], bcum[lo:lo + sub]
        a = jnp.zeros((sub, LANES), F32)
        if i > 0:
            ref = bcum[lo - 1:lo]
            qd = qi * jnp.exp(bi - ref)
            kd = k * jnp.exp(jnp.where(row < lo, ref - bcum, -jnp.inf))
            a = _dot_nt(qd.astype(BF16), kd.astype(BF16))
            if rows < LANES:
                a = jnp.concatenate([a, jnp.zeros((sub, LANES - rows), F32)], axis=1)
        for s in range(sub):
            d = jnp.where(srow >= s, bi - bi[s:s + 1], -jnp.inf)
            col = jnp.sum(qi * jnp.exp(d) * ki[s:s + 1], axis=-1, keepdims=True)
            a = jnp.where(lane == lo + s, col, a)
        out.append(a)
    return out[0] if len(out) == 1 else jnp.concatenate(out, axis=0)


def _pad_rows(x, rows):
    if x.shape[0] == rows:
        return x
    return jnp.concatenate([x, jnp.zeros((rows - x.shape[0], x.shape[1]), x.dtype)], axis=0)


def _gla_chunk(st, q, k, v, g, sub, last):
    bcum = _cumsum_rows(g)
    b_last = bcum[last:last + 1]
    o = _dot_nt((q * jnp.exp(bcum)).astype(BF16), st.astype(BF16))
    a = _gla_scores(q, k, bcum, sub)
    v128 = _pad_rows(v, LANES).astype(BF16)
    o = o + _dot(a.astype(BF16), v128)
    kdec = _pad_rows(k * jnp.exp(b_last - bcum), LANES).astype(BF16)
    vt = _pad_rows(v, LANES).T.astype(BF16)
    st_new = st * jnp.exp(b_last) + _dot(vt, kdec)
    return o, st_new


def _gla_finish(o, gate, ng_ref):
    return (_rms(o, ng_ref[...]) * (gate * _sigmoid(gate))).astype(BF16)


def _hgrn_prompt_kernel(lbp_ref, hq_ref, hf_ref, hi_ref, hg_ref, ng_ref, y_ref, sfin_ref, st_scr, *, layer):
    r = pl.program_id(2)
    c = CHUNK_B

    @pl.when(r == 0)
    def _():
        st_scr[...] = jnp.zeros(st_scr.shape, F32)

    lb = _lower_bound(lbp_ref, layer)

    def body(ci, carry):
        r0 = pl.multiple_of(ci * c, c)
        rows = pl.ds(r0, c)
        q, k, g = _gla_gates(hq_ref[rows, :], hf_ref[rows, :], lb)
        o, st_new = _gla_chunk(st_scr[...], q, k, hi_ref[rows, :], g, SUB_B, c - 1)
        st_scr[...] = st_new
        y_ref[rows, :] = _gla_finish(o, hg_ref[rows, :], ng_ref)
        return carry

    lax.fori_loop(0, hq_ref.shape[0] // c, body, 0)

    @pl.when(r == pl.num_programs(2) - 1)
    def _():
        sfin_ref[0, 0] = st_scr[...].T


def hgrn_prompt(h_in, nb, seq, lb_param, norm_g, layer):
    nh = h_in.shape[1] // (4 * LANES)
    rows = min(ROWS_B, seq)
    nr = seq // rows
    depth = lb_param.shape[0]
    kern = functools.partial(_hgrn_prompt_kernel, layer=layer)

    def col(j):
        return pl.BlockSpec((rows, LANES), lambda b, h, r: (b * nr + r, j * nh + h))

    return pl.pallas_call(
        kern,
        grid=(nb, nh, nr),
        in_specs=[pl.BlockSpec((depth, LANES), lambda b, h, r: (0, h)),
                  col(0), col(1), col(2), col(3),
                  pl.BlockSpec((1, LANES), lambda b, h, r: (0, 0))],
        out_specs=[pl.BlockSpec((rows, LANES), lambda b, h, r: (b * nr + r, h)),
                   pl.BlockSpec((1, 1, LANES, LANES), lambda b, h, r: (b, h, 0, 0))],
        out_shape=[jax.ShapeDtypeStruct((nb * seq, nh * LANES), BF16),
                   jax.ShapeDtypeStruct((nb, nh, LANES, LANES), F32)],
        scratch_shapes=[pltpu.VMEM((LANES, LANES), F32)],
        compiler_params=_cparams(("parallel", "parallel", "arbitrary"), 32),
        name="hgrn_prompt",
    )(lb_param, h_in, h_in, h_in, h_in, norm_g.reshape(1, LANES))


def _hgrn_sample_kernel(lbp_ref, h_ref, ng_ref, s0_ref, y_ref, s1_ref, *, layer, tdec):
    nh = s0_ref.shape[1]
    rows = h_ref.shape[1]
    for hd in range(nh):
        def cols(j, hd=hd):
            return slice((j * nh + hd) * LANES, (j * nh + hd + 1) * LANES)
        lb = _lower_bound(lbp_ref.at[:, cols(0)], layer)
        q, k, g = _gla_gates(h_ref[0, :, cols(0)], h_ref[0, :, cols(1)], lb)
        o, st_new = _gla_chunk(s0_ref[0, hd].T, q, k, h_ref[0, :, cols(2)], g, rows, tdec - 1)
        y_ref[0, :, cols(0)] = _gla_finish(o, h_ref[0, :, cols(3)], ng_ref)
        s1_ref[0, hd] = st_new.T


def hgrn_sample(h_in, state, state_layer, lb_param, norm_g, layer, tdec):
    nb, rows, width = h_in.shape
    nh = width // (4 * LANES)
    depth = lb_param.shape[0]
    kern = functools.partial(_hgrn_sample_kernel, layer=layer, tdec=tdec)
    return pl.pallas_call(
        kern,
        grid=(nb,),
        in_specs=[pl.BlockSpec((depth, nh * LANES), lambda b: (0, 0)),
                  pl.BlockSpec((1, rows, width), lambda b: (b, 0, 0)),
                  pl.BlockSpec((1, LANES), lambda b: (0, 0)),
                  pl.BlockSpec((None, 1, nh, LANES, LANES), lambda b: (state_layer, b, 0, 0, 0))],
        out_specs=[pl.BlockSpec((1, rows, nh * LANES), lambda b: (b, 0, 0)),
                   pl.BlockSpec((1, nh, LANES, LANES), lambda b: (b, 0, 0, 0))],
        out_shape=[jax.ShapeDtypeStruct((nb, rows, nh * LANES), BF16),
                   jax.ShapeDtypeStruct((nb, nh, LANES, LANES), F32)],
        compiler_params=_cparams(("parallel",), 32),
        name="hgrn_sample",
    )(lb_param, h_in, norm_g.reshape(1, LANES), state)


def _row_tile(m):
    for t in (640, 512, 256, 128, 64, 32, 16, 8):
        if m % t == 0:
            return t
    raise ValueError(f"unsupported row count {m}")


def _col_tile(n, cands):
    for t in cands:
        if n % t == 0:
            return t
    raise ValueError(f"unsupported column count {n}")


def kernel(x_prompt, x_sample, cache_kv_a, state_hgrn_b, cache_win_c, page_table, norm_g, a_w_in, a_w_out, a_lambda, a_subln_g, b_w_in, b_w_out, b_lower_bound, b_norm_g, c_w_in, c_w_out, c_sinks, rel_bias_table, ffn_w_gate_up, ffn_w_down):
    nb, seq, d = x_prompt.shape
    nbs, tdec, _ = x_sample.shape
    depth = norm_g.shape[0]
    mp, ms = nb * seq, nbs * tdec
    m = mp + ms
    tm = _row_tile(m)
    page = cache_kv_a.shape[2]
    assert page == LANES and page >= MAX_DISTANCE and tdec <= 8
    kvw = N_KV_A * 2 * HEAD_DIM_A
    cache = cache_kv_a.reshape(cache_kv_a.shape[0], cache_kv_a.shape[1], page * 2 * N_KV_A, 2 * HEAD_DIM_A)
    table = rel_bias_table.astype(F32)
    ga, gc = GROUP_A, GROUP_C
    wb = cache_win_c.shape[2]
    keys_c = 2 * WINDOW

    dist_a = _band_dist(TQ_A)
    bias_a = bias_tiles(table, dist_a, dist_a >= 0)
    dist_c = _band_dist(WINDOW)
    bias_c = bias_tiles(table, dist_c, (dist_c >= 0) & (dist_c <= WINDOW))

    dist_as, in_as = _decode_dist(tdec, page, 2 * LANES)
    bias_as = bias_tiles(table, dist_as, in_as & (dist_as >= 0))
    bias_as = bias_as.reshape(N_KV_A, ga, 2, 8, 2 * LANES).transpose(0, 2, 1, 3, 4)[:, :, :, :tdec]
    bias_as = bias_as.reshape(N_KV_A, 2 * ga * tdec, 2 * LANES)
    last_a, new_a = bias_as[:, :, :LANES], bias_as[:, :, LANES:]
    far_a = table[NUM_BUCKETS - 1].reshape(N_KV_A, ga, 2).transpose(0, 2, 1)
    far_a = jnp.broadcast_to(far_a[:, :, :, None, None], (N_KV_A, 2, ga, tdec, LANES))
    far_a = far_a.reshape(N_KV_A, 2 * ga * tdec, LANES)
    dist_cs, in_cs = _decode_dist(tdec, wb, keys_c)
    bias_cs = bias_tiles(table, dist_cs, in_cs & (dist_cs >= 0) & (dist_cs <= WINDOW))
    bias_cs = bias_cs.reshape(N_KV_C, gc, 8, keys_c)[:, :, :tdec].reshape(N_KV_C, gc * tdec, keys_c)

    h = jnp.concatenate([x_prompt.reshape(mp, d), x_sample.reshape(ms, d)], axis=0)
    kv_p, kv_s, hg_p, hg_s, win_p, win_s = [], [], [], [], [], []
    for layer in range(depth):
        kind, j = layer % 3, layer // 3
        g = norm_g[layer]
        if kind == 0:
            lam_init = 0.8 - 0.6 * math.exp(-0.3 * layer)
            w_in = a_w_in[j].astype(BF16)
            qkv = norm_matmul(h, g[0], w_in, tm=tm, tn=_col_tile(w_in.shape[1], (1024, 512, 256, 128)))
            dq = N_KV_A * GROUP_A * 2 * HEAD_DIM_A
            lam_q, lam_k = a_lambda[j][0::2], a_lambda[j][1::2]
            yp = attn_a_prompt(qkv, nb, seq, table, bias_a, lam_q, lam_k, a_subln_g[j], lam_init)
            qs = qkv[mp:, :dq].reshape(nbs, tdec, N_KV_A, ga, LANES).transpose(0, 2, 3, 1, 4)
            qs = qs.reshape(nbs, N_KV_A, ga * tdec, LANES)
            kvs = qkv[mp:, dq:].reshape(nbs, tdec, 2 * kvw)
            ys = attn_a_sample(qs, cache, j, page_table, kvs, far_a, last_a, new_a,
                               lam_q, lam_k, a_subln_g[j], lam_init)
            ys = ys.reshape(nbs, N_KV_A, ga, tdec, LANES).transpose(0, 3, 1, 2, 4).reshape(ms, dq)
            kv_p.append(qkv[:mp, dq:].reshape(nb, seq, 2, N_KV_A, 2 * HEAD_DIM_A))
            kv_s.append(kvs.reshape(nbs, tdec, 2, N_KV_A, 2 * HEAD_DIM_A))
            w_out = a_w_out[j]
        elif kind == 1:
            w_in = b_w_in[j].astype(BF16)
            hin = norm_matmul(h, g[0], w_in, tm=tm, tn=_col_tile(w_in.shape[1], (1024, 512, 256, 128)))
            yp, sp = hgrn_prompt(hin, nb, seq, b_lower_bound, b_norm_g[j], layer)
            hs_in = jnp.pad(hin[mp:].reshape(nbs, tdec, hin.shape[1]), ((0, 0), (0, 8 - tdec), (0, 0)))
            ys, ss = hgrn_sample(hs_in, state_hgrn_b, j, b_lower_bound, b_norm_g[j], layer, tdec)
            ys = ys[:, :tdec].reshape(ms, d)
            hg_p.append(sp)
            hg_s.append(ss)
            w_out = b_w_out[j]
        else:
            w_in = c_w_in[j].astype(BF16)
            qkv = norm_matmul(h, g[0], w_in, tm=tm, tn=_col_tile(w_in.shape[1], (1280, 512, 256, 128)))
            dq = N_KV_C * GROUP_C * HEAD_DIM_C
            dkv = N_KV_C * HEAD_DIM_C
            yp = swa_prompt(qkv, nb, seq, c_sinks[j], bias_c)
            kv_new = qkv[mp:, dq:].reshape(nbs, tdec, 2 * dkv)
            kv_all = jnp.concatenate([cache_win_c[j].reshape(nbs, wb, 2 * dkv), kv_new], axis=1)
            kv_pad = jnp.pad(kv_all, ((0, 0), (0, keys_c - wb - tdec), (0, 0)))
            kv_r = kv_pad.reshape(nbs, keys_c, 2, N_KV_C, HEAD_DIM_C).transpose(2, 0, 3, 1, 4)
            qs = qkv[mp:, :dq].reshape(nbs, tdec, N_KV_C, gc, HEAD_DIM_C).transpose(0, 2, 3, 1, 4)
            qs = qs.reshape(nbs, N_KV_C, gc * tdec, HEAD_DIM_C)
            sink_rows = jnp.broadcast_to(c_sinks[j].reshape(N_KV_C, gc, 1, 1), (N_KV_C, gc, tdec, LANES))
            sink_rows = sink_rows.reshape(N_KV_C, gc * tdec, LANES)
            ys = swa_sample(qs, kv_r[0], kv_r[1], bias_cs, sink_rows)
            ys = ys.reshape(nbs, N_KV_C, gc, tdec, HEAD_DIM_C).transpose(0, 3, 1, 2, 4).reshape(ms, dq)
            win_p.append(qkv[:mp, dq:].reshape(nb, seq, 2, N_KV_C, HEAD_DIM_C)[:, seq - wb:])
            win_s.append(kv_all[:, tdec:].reshape(nbs, wb, 2, N_KV_C, HEAD_DIM_C))
            w_out = c_w_out[j]
        y = jnp.concatenate([yp, ys], axis=0)
        h = matmul_postnorm_residual(y, w_out.astype(BF16), g[1], h, tm=tm)
        w_gu = ffn_w_gate_up[layer].astype(BF16)
        dff = w_gu.shape[1] // 2
        hm = norm_swiglu(h, g[2], w_gu, tm=tm, tn=_col_tile(dff, (512, 256, 128)))
        h = matmul_postnorm_residual(hm, ffn_w_down[layer].astype(BF16), g[3], h, tm=tm)
    return (h[:mp].reshape(nb, seq, d), h[mp:].reshape(nbs, tdec, d),
            jnp.stack(kv_p), jnp.stack(kv_s), jnp.stack(hg_p), jnp.stack(hg_s),
            jnp.stack(win_p), jnp.stack(win_s))
```

```python
import functools
import math

import numpy as np
import jax
import jax.numpy as jnp
from jax import lax
from jax.experimental import pallas as pl
from jax.experimental.pallas import tpu as pltpu

F32 = jnp.float32
BF16 = jnp.bfloat16

NORM_EPS = 1e-6
NUM_BUCKETS = 32
MAX_DISTANCE = 128
NEG = -1e30
LANES = 128
HALF = LANES // 2
MIB = 1024 * 1024

HEAD_DIM_A = 64
N_KV_A = 4
GROUP_A = 4
HEAD_DIM_C = 64
N_KV_C = 4
GROUP_C = 8
WINDOW = 128
TQ_A = 256
PAGES_PER_STEP = 16
CHUNK_B = 128
SUB_B = 16
ROWS_B = 512


def _cparams(sem, vmem_mib):
    return pltpu.CompilerParams(dimension_semantics=sem, vmem_limit_bytes=vmem_mib * MIB)


def _rms(x, g):
    ms = jnp.mean(x * x, axis=-1, keepdims=True)
    return x * lax.rsqrt(ms + NORM_EPS) * g


def _sigmoid(x):
    return 1.0 / (1.0 + jnp.exp(-x))


def _dot(a, b):
    return jnp.dot(a, b, preferred_element_type=F32)


def _dot_nt(a, b):
    return lax.dot_general(a, b, (((1,), (1,)), ((), ())), preferred_element_type=F32)


def _norm_matmul_kernel(x_ref, g_ref, w_ref, o_ref, *rest, n_tail):
    xn_ref = rest[-1]
    j = pl.program_id(1)

    @pl.when(j == 0)
    def _():
        xn_ref[...] = _rms(x_ref[...], g_ref[...]).astype(BF16)

    res = _dot(xn_ref[...], w_ref[...])
    o_ref[...] = res
    if n_tail:
        tail_ref = rest[0]
        tm = o_ref.shape[0]

        @pl.when(j == pl.num_programs(1) - 1)
        def _():
            for slot in range(n_tail):
                tail_ref[pl.ds(slot, tm, stride=n_tail), :] = res[:, slot * LANES:(slot + 1) * LANES]


def norm_matmul(x, g, w, layer, *, tm, tn, tail_rows=False):
    m, k = x.shape
    n = w.shape[2]
    assert m % tm == 0 and n % tn == 0
    n_tail = tn // LANES if tail_rows else 0
    out_specs = [pl.BlockSpec((tm, tn), lambda i, j: (i, j))]
    out_shape = [jax.ShapeDtypeStruct((m, n), F32)]
    if tail_rows:
        out_specs.append(pl.BlockSpec((tm * n_tail, LANES), lambda i, j: (i, 0)))
        out_shape.append(jax.ShapeDtypeStruct((m * n_tail, LANES), F32))
    out = pl.pallas_call(
        functools.partial(_norm_matmul_kernel, n_tail=n_tail),
        grid=(m // tm, n // tn),
        in_specs=[pl.BlockSpec((tm, k), lambda i, j: (i, 0)),
                  pl.BlockSpec((1, k), lambda i, j: (0, 0)),
                  pl.BlockSpec((None, k, tn), lambda i, j: (layer, 0, j))],
        out_specs=out_specs,
        out_shape=out_shape,
        scratch_shapes=[pltpu.VMEM((tm, k), BF16)],
        compiler_params=_cparams(("parallel", "arbitrary"), 56),
        name="norm_matmul",
    )(x, g.reshape(1, k), w)
    return out if tail_rows else out[0]


def _norm_swiglu_kernel(x_ref, g_ref, wg_ref, wu_ref, o_ref, xn_ref):
    @pl.when(pl.program_id(1) == 0)
    def _():
        xn_ref[...] = _rms(x_ref[...], g_ref[...]).astype(BF16)

    xn = xn_ref[...]
    gate = _dot(xn, wg_ref[...].astype(BF16))
    up = _dot(xn, wu_ref[...].astype(BF16))
    o_ref[...] = (gate * _sigmoid(gate) * up).astype(o_ref.dtype)


def norm_swiglu(x, g, w_gate_up, layer, *, tm, tn):
    m, k = x.shape
    dff = w_gate_up.shape[2] // 2
    assert m % tm == 0 and dff % tn == 0
    nj = dff // tn
    return pl.pallas_call(
        _norm_swiglu_kernel,
        grid=(m // tm, nj),
        in_specs=[pl.BlockSpec((tm, k), lambda i, j: (i, 0)),
                  pl.BlockSpec((1, k), lambda i, j: (0, 0)),
                  pl.BlockSpec((None, k, tn), lambda i, j: (layer, 0, j)),
                  pl.BlockSpec((None, k, tn), lambda i, j: (layer, 0, j + nj))],
        out_specs=pl.BlockSpec((tm, tn), lambda i, j: (i, j)),
        out_shape=jax.ShapeDtypeStruct((m, dff), BF16),
        scratch_shapes=[pltpu.VMEM((tm, k), BF16)],
        compiler_params=_cparams(("parallel", "arbitrary"), 56),
        name="norm_swiglu",
    )(x, g.reshape(1, k), w_gate_up, w_gate_up)


def _matmul_postnorm_kernel(y_ref, w_ref, g_ref, h_ref, o_ref, acc_ref, *, nk):
    def finish(acc):
        o_ref[...] = h_ref[...] + _rms(acc, g_ref[...])

    if nk == 1:
        finish(_dot(y_ref[...], w_ref[...]))
        return
    k = pl.program_id(1)

    @pl.when(k == 0)
    def _():
        acc_ref[...] = _dot(y_ref[...], w_ref[...])

    @pl.when(jnp.logical_and(k > 0, k < nk - 1))
    def _():
        acc_ref[...] += _dot(y_ref[...], w_ref[...])

    @pl.when(k == nk - 1)
    def _():
        finish(acc_ref[...] + _dot(y_ref[...], w_ref[...]))


def matmul_postnorm_residual(y, w, layer, g, h, *, tm):
    m, kdim = y.shape
    n = w.shape[2]
    tk = kdim if kdim <= 2048 else _col_tile(kdim, (1408, 1024, 512, 256, 128))
    assert m % tm == 0 and kdim % tk == 0
    nk = kdim // tk
    return pl.pallas_call(
        functools.partial(_matmul_postnorm_kernel, nk=nk),
        grid=(m // tm, nk),
        in_specs=[pl.BlockSpec((tm, tk), lambda i, k: (i, k)),
                  pl.BlockSpec((None, tk, n), lambda i, k: (layer, k, 0)),
                  pl.BlockSpec((1, n), lambda i, k: (0, 0)),
                  pl.BlockSpec((tm, n), lambda i, k: (i, 0))],
        out_specs=pl.BlockSpec((tm, n), lambda i, k: (i, 0)),
        out_shape=jax.ShapeDtypeStruct((m, n), F32),
        scratch_shapes=[pltpu.VMEM((tm, n) if nk > 1 else (8, LANES), F32)],
        compiler_params=_cparams(("parallel", "arbitrary"), 56),
        name="matmul_postnorm_residual",
    )(y, w, g.reshape(1, n), h)


def _bucket_np(dist):
    n = np.maximum(dist, 0)
    max_exact = NUM_BUCKETS // 2
    ratio = np.log(np.maximum(n, 1).astype(np.float32) / np.float32(max_exact)) / np.float32(
        math.log(MAX_DISTANCE / max_exact))
    large = np.minimum(max_exact + (ratio * (NUM_BUCKETS - max_exact)).astype(np.int32), NUM_BUCKETS - 1)
    return np.where(n < max_exact, n, large).astype(np.int32)


def _bias_tile_kernel(tab_ref, bucket_ref, mask_ref, o_ref):
    c = pl.program_id(0)
    b = bucket_ref[...]
    acc = jnp.zeros(b.shape, F32)
    for k in range(NUM_BUCKETS):
        acc = jnp.where(b == k, tab_ref[k, c], acc)
    o_ref[0] = jnp.where(mask_ref[...] > 0, acc, NEG)


def bias_tiles(table, dist, valid):
    ncol = table.shape[1]
    r, c = dist.shape
    return pl.pallas_call(
        _bias_tile_kernel,
        grid=(ncol,),
        in_specs=[pl.BlockSpec(memory_space=pltpu.SMEM),
                  pl.BlockSpec((r, c), lambda i: (0, 0)),
                  pl.BlockSpec((r, c), lambda i: (0, 0))],
        out_specs=pl.BlockSpec((1, r, c), lambda i: (i, 0, 0)),
        out_shape=jax.ShapeDtypeStruct((ncol, r, c), F32),
        compiler_params=_cparams(("parallel",), 32),
        name="bias_tiles",
    )(table, jnp.asarray(_bucket_np(dist)), jnp.asarray(valid.astype(np.int32)))


def _band_dist(t):
    return np.arange(t)[:, None] + t - np.arange(2 * t)[None, :]


def _decode_dist(tdec, n_past, cols):
    t = np.arange(8)[:, None]
    k = np.arange(cols)[None, :]
    dist = np.where(k < n_past, t + n_past - k, t - (k - n_past))
    inside = (k < n_past + tdec) & (t < tdec)
    return dist, inside


def _diff_lambda(lq_ref, lk_ref, lam_init):
    e = jnp.exp(jnp.sum(lq_ref[...] * lk_ref[...], axis=-1, keepdims=True))
    return e[0:1] - e[1:2] + lam_init


def _split_maps(q):
    lane = lax.broadcasted_iota(jnp.int32, q.shape, 1)
    return jnp.where(lane < HALF, q, 0.0), jnp.where(lane >= HALF, q, 0.0)


def _attn_a_prompt_kernel(tab_ref, lq_ref, lk_ref, q_ref, k_ref, v_ref, bias_ref, sg_ref, o_ref,
                          k_scr, v_scr, s_scr, m_scr, l_scr, acc_scr, *, lam_init, scale):
    kvh = pl.program_id(0)
    qi = pl.program_id(2)
    tq = q_ref.shape[0]
    nt = k_scr.shape[0]
    ng = GROUP_A
    nrg = 2 * ng

    @pl.when(qi == 0)
    def _():
        for t in range(nt):
            k_scr[t] = k_ref[t * tq:(t + 1) * tq, :].astype(BF16)
            v_scr[t] = v_ref[t * tq:(t + 1) * tq, :].astype(BF16)

    q = q_ref[...] * scale
    parts1, parts2 = [], []
    for g in range(ng):
        q1, q2 = _split_maps(q[:, g * LANES:(g + 1) * LANES])
        parts1.append(q1)
        parts2.append(q2)
    qs = jnp.concatenate(parts1 + parts2, axis=0).astype(BF16)

    def score_tile(t, bias_of_group, first=False):
        s = _dot_nt(qs, k_scr[t])
        for rg in range(nrg):
            rows = slice(rg * tq, (rg + 1) * tq)
            sg = s[rows] + bias_of_group(rg)
            s_scr[t, rows, :] = sg
            mx = sg[:, :LANES]
            for c in range(1, tq // LANES):
                mx = jnp.maximum(mx, sg[:, c * LANES:(c + 1) * LANES])
            m_scr[rows, :] = mx if first else jnp.maximum(m_scr[rows, :], mx)

    def col_of_group(rg):
        m, g = divmod(rg, ng)
        return 2 * g + m

    score_tile(qi, lambda rg: bias_ref[col_of_group(rg), :, tq:2 * tq], first=True)

    def far_body(t, carry):
        score_tile(t, lambda rg: tab_ref[NUM_BUCKETS - 1, kvh * nrg + col_of_group(rg)])
        return carry

    lax.fori_loop(0, qi - 1, far_body, 0)

    @pl.when(qi >= 1)
    def _():
        score_tile(qi - 1, lambda rg: bias_ref[col_of_group(rg), :, 0:tq])

    m_row = jnp.max(m_scr[...], axis=-1, keepdims=True)

    def pv_tile(t, first=False):
        p = jnp.exp(s_scr[t] - m_row)
        ps = p[:, :LANES]
        for c in range(1, tq // LANES):
            ps = ps + p[:, c * LANES:(c + 1) * LANES]
        pv = _dot(p.astype(BF16), v_scr[t])
        if first:
            l_scr[...] = ps
            acc_scr[...] = pv
        else:
            l_scr[...] += ps
            acc_scr[...] += pv

    pv_tile(qi, first=True)

    def pv_body(t, carry):
        pv_tile(t)
        return carry

    lax.fori_loop(0, qi, pv_body, 0)

    l_row = jnp.sum(l_scr[...], axis=-1, keepdims=True)
    o_all = acc_scr[...] / l_row
    half = ng * tq
    lam = _diff_lambda(lq_ref, lk_ref, lam_init)
    o = o_all[:half] - lam * o_all[half:]
    for g in range(ng):
        og = _rms(o[g * tq:(g + 1) * tq], sg_ref[...]) * (1.0 - lam_init)
        o_ref[:, g * LANES:(g + 1) * LANES] = og.astype(o_ref.dtype)


def attn_a_prompt(qkv, nb, seq, table, bias_near, lam_q, lam_k, subln_g, lam_init):
    tq = TQ_A
    nq = seq // tq
    nqb = (N_KV_A * GROUP_A * LANES) // LANES
    kern = functools.partial(_attn_a_prompt_kernel, lam_init=lam_init, scale=HEAD_DIM_A ** -0.5)
    nrow = 2 * GROUP_A * tq
    return pl.pallas_call(
        kern,
        grid=(N_KV_A, nb, nq),
        in_specs=[pl.BlockSpec(memory_space=pltpu.SMEM),
                  pl.BlockSpec((2, HEAD_DIM_A), lambda h, b, i: (0, 0)),
                  pl.BlockSpec((2, HEAD_DIM_A), lambda h, b, i: (0, 0)),
                  pl.BlockSpec((tq, GROUP_A * LANES), lambda h, b, i: (b * nq + i, h)),
                  pl.BlockSpec((seq, LANES), lambda h, b, i: (b, nqb + h)),
                  pl.BlockSpec((seq, LANES), lambda h, b, i: (b, nqb + N_KV_A + h)),
                  pl.BlockSpec((2 * GROUP_A, tq, 2 * tq), lambda h, b, i: (h, 0, 0)),
                  pl.BlockSpec((1, LANES), lambda h, b, i: (0, 0))],
        out_specs=pl.BlockSpec((tq, GROUP_A * LANES), lambda h, b, i: (b * nq + i, h)),
        out_shape=jax.ShapeDtypeStruct((nb * seq, N_KV_A * GROUP_A * LANES), BF16),
        scratch_shapes=[pltpu.VMEM((nq, tq, LANES), BF16),
                        pltpu.VMEM((nq, tq, LANES), BF16),
                        pltpu.VMEM((nq, nrow, tq), F32),
                        pltpu.VMEM((nrow, LANES), F32),
                        pltpu.VMEM((nrow, LANES), F32),
                        pltpu.VMEM((nrow, LANES), F32)],
        compiler_params=_cparams(("parallel", "parallel", "arbitrary"), 56),
        name="attn_a_prompt",
    )(table, lam_q, lam_k, qkv, qkv, qkv, bias_near, subln_g.reshape(1, LANES))


def _attn_a_sample_kernel(pt_ref, lq_ref, lk_ref, q_ref, *rest, lam_init, scale, npp):
    page_refs = rest[:npp]
    (knew_ref, bfar_ref, blast_ref, bnew_ref, sg_ref, o_ref, m_scr, l_scr, acc_scr) = rest[npp:]
    ps = pl.program_id(1)
    nsteps = pl.num_programs(1)
    nkv = N_KV_A
    tdec = knew_ref.shape[1]

    @pl.when(ps == 0)
    def _():
        m_scr[...] = jnp.full(m_scr.shape, -jnp.inf, F32)
        l_scr[...] = jnp.zeros(l_scr.shape, F32)
        acc_scr[...] = jnp.zeros(acc_scr.shape, F32)

    qs32 = []
    for h in range(nkv):
        q1, q2 = _split_maps(q_ref[0, h] * scale)
        qs32.append(jnp.concatenate([q1, q2], axis=0))
    qs = [x.astype(BF16) for x in qs32]
    is_last = ps == nsteps - 1

    nslot = 2 * nkv
    page = page_refs[0].shape[0] // nslot

    def slot_rows(slot):
        rows = [r[pl.ds(slot, page, stride=nslot), :] for r in page_refs]
        return jnp.concatenate(rows, axis=0).astype(BF16)

    for h in range(nkv):
        kh = slot_rows(h)
        vh = slot_rows(nkv + h)
        s = _dot_nt(qs[h], kh)
        bfar = bfar_ref[h]
        tail = jnp.where(is_last, blast_ref[h], bfar)
        s = jnp.concatenate([s[:, :-page] + bfar[:, 0:1], s[:, -page:] + tail], axis=1)
        m_prev = m_scr[h]
        m_new = jnp.maximum(m_prev, jnp.max(s, axis=-1, keepdims=True))
        alpha = jnp.exp(m_prev - m_new)
        p = jnp.exp(s - m_new[:, 0:1])
        l_scr[h] = alpha * l_scr[h] + jnp.sum(p, axis=-1, keepdims=True)
        acc_scr[h] = alpha * acc_scr[h] + _dot(p.astype(BF16), vh)
        m_scr[h] = m_new

    @pl.when(is_last)
    def _():
        lam = _diff_lambda(lq_ref, lk_ref, lam_init)
        knew = knew_ref[0]
        for h in range(nkv):
            kn = knew[:, h * LANES:(h + 1) * LANES]
            vn = knew[:, (nkv + h) * LANES:(nkv + h + 1) * LANES]
            bnew = bnew_ref[h]
            cols = [jnp.sum(qs32[h] * kn[t:t + 1, :], axis=-1, keepdims=True) + bnew[:, t:t + 1]
                    for t in range(tdec)]
            m_prev = m_scr[h]
            m_new = m_prev
            for c in cols:
                m_new = jnp.maximum(m_new, c)
            alpha = jnp.exp(m_prev - m_new)
            l = alpha * l_scr[h]
            acc = alpha * acc_scr[h]
            for t in range(tdec):
                p = jnp.exp(cols[t] - m_new)
                l = l + p
                acc = acc + p * vn[t:t + 1, :]
            o_all = acc / l
            half = o_all.shape[0] // 2
            o = o_all[:half] - lam * o_all[half:]
            o_ref[0, h] = (_rms(o, sg_ref[...]) * (1.0 - lam_init)).astype(o_ref.dtype)


def attn_a_sample(q_r, cache, layer, page_table, knew, bias_far, bias_last, bias_new,
                  lam_q, lam_k, subln_g, lam_init):
    nb, nkv, ngt, _ = q_r.shape
    n_pages = page_table.shape[1]
    page_rows = cache.shape[2]
    assert page_rows == 2 * nkv * LANES and cache.shape[3] == LANES
    npp = min(PAGES_PER_STEP, n_pages)
    assert n_pages % npp == 0
    tdec = knew.shape[1]
    width = knew.shape[2]
    kern = functools.partial(_attn_a_sample_kernel, lam_init=lam_init, scale=HEAD_DIM_A ** -0.5, npp=npp)

    def page_spec(i):
        return pl.BlockSpec((None, None, page_rows, LANES),
                            lambda b, p, pt: (layer, pt[b, p * npp + i], 0, 0))

    const3 = lambda b, p, pt: (0, 0, 0)
    grid_spec = pltpu.PrefetchScalarGridSpec(
        num_scalar_prefetch=1,
        grid=(nb, n_pages // npp),
        in_specs=[pl.BlockSpec((2, HEAD_DIM_A), lambda b, p, pt: (0, 0)),
                  pl.BlockSpec((2, HEAD_DIM_A), lambda b, p, pt: (0, 0)),
                  pl.BlockSpec((1, nkv, ngt, LANES), lambda b, p, pt: (b, 0, 0, 0))]
                 + [page_spec(i) for i in range(npp)]
                 + [pl.BlockSpec((1, tdec, width), lambda b, p, pt: (b, 0, 0)),
                    pl.BlockSpec((nkv, 2 * ngt, LANES), const3),
                    pl.BlockSpec((nkv, 2 * ngt, LANES), const3),
                    pl.BlockSpec((nkv, 2 * ngt, LANES), const3),
                    pl.BlockSpec((1, LANES), lambda b, p, pt: (0, 0))],
        out_specs=pl.BlockSpec((1, nkv, ngt, LANES), lambda b, p, pt: (b, 0, 0, 0)),
        scratch_shapes=[pltpu.VMEM((nkv, 2 * ngt, LANES), F32),
                        pltpu.VMEM((nkv, 2 * ngt, LANES), F32),
                        pltpu.VMEM((nkv, 2 * ngt, LANES), F32)],
    )
    return pl.pallas_call(
        kern,
        grid_spec=grid_spec,
        out_shape=jax.ShapeDtypeStruct((nb, nkv, ngt, LANES), BF16),
        compiler_params=_cparams(("parallel", "arbitrary"), 48),
        name="attn_a_sample",
    )(page_table, lam_q, lam_k, q_r, *([cache] * npp), knew, bias_far, bias_last, bias_new,
      subln_g.reshape(1, LANES))


def _dup_half(x, odd):
    lane = lax.broadcasted_iota(jnp.int32, x.shape, 1)
    rolled = pltpu.roll(x, HALF, axis=1)
    keep = (lane >= HALF) if odd else (lane < HALF)
    return jnp.where(keep, x, rolled)


def _swa_prompt_kernel(sink_ref, q_ref, kvp_ref, kvo_ref, bias_ref, o_ref, *, scale):
    n = pl.program_id(1)
    w = q_ref.shape[0]
    ng = GROUP_C
    kv = jnp.concatenate([kvp_ref[...], kvo_ref[...]], axis=0)
    q = q_ref[...] * scale
    lane = lax.broadcasted_iota(jnp.int32, (w, LANES), 1)
    kcol = lax.broadcasted_iota(jnp.int32, (ng * w, 2 * w), 1)
    first = n == 0
    vpair0 = (N_KV_C * HEAD_DIM_C) // LANES
    for h in range(N_KV_C):
        kd = _dup_half(kv[:, (h // 2) * LANES:(h // 2 + 1) * LANES], h % 2).astype(BF16)
        vd = _dup_half(kv[:, (vpair0 + h // 2) * LANES:(vpair0 + h // 2 + 1) * LANES], h % 2).astype(BF16)
        parts, sinks = [], []
        for g in range(ng):
            qp = q[:, (h * ng // 2 + g // 2) * LANES:(h * ng // 2 + g // 2 + 1) * LANES]
            keep = (lane >= HALF) if g % 2 else (lane < HALF)
            parts.append(jnp.where(keep, qp, 0.0))
            sinks.append(jnp.full((w, 1), sink_ref[h * ng + g], F32))
        qs = jnp.concatenate(parts, axis=0).astype(BF16)
        sink = jnp.concatenate(sinks, axis=0)
        bias = bias_ref[h * ng:(h + 1) * ng].reshape(ng * w, 2 * w)
        bias = jnp.where(jnp.logical_and(first, kcol < w), NEG, bias)
        s = _dot_nt(qs, kd) + bias
        m = jnp.maximum(jnp.max(s, axis=-1, keepdims=True), sink)
        p = jnp.exp(s - m)
        den = jnp.sum(p, axis=-1, keepdims=True) + jnp.exp(sink - m)
        o = _dot(p.astype(BF16), vd) / den
        for j in range(ng // 2):
            oe = o[(2 * j) * w:(2 * j + 1) * w]
            oo = o[(2 * j + 1) * w:(2 * j + 2) * w]
            col = (h * ng // 2 + j) * LANES
            o_ref[:, col:col + LANES] = jnp.where(lane < HALF, oe, oo).astype(o_ref.dtype)


def swa_prompt(qkv, nb, seq, sinks, bias_swa):
    w = WINDOW
    nblk = seq // w
    dq = N_KV_C * GROUP_C * HEAD_DIM_C
    dkv = 2 * N_KV_C * HEAD_DIM_C
    kern = functools.partial(_swa_prompt_kernel, scale=HEAD_DIM_C ** -0.5)
    return pl.pallas_call(
        kern,
        grid=(nb, nblk),
        in_specs=[pl.BlockSpec(memory_space=pltpu.SMEM),
                  pl.BlockSpec((w, dq), lambda b, n: (b * nblk + n, 0)),
                  pl.BlockSpec((w, dkv), lambda b, n: (b * nblk + jnp.maximum(n - 1, 0), dq // dkv)),
                  pl.BlockSpec((w, dkv), lambda b, n: (b * nblk + n, dq // dkv)),
                  pl.BlockSpec((N_KV_C * GROUP_C, w, 2 * w), lambda b, n: (0, 0, 0))],
        out_specs=pl.BlockSpec((w, dq), lambda b, n: (b * nblk + n, 0)),
        out_shape=jax.ShapeDtypeStruct((nb * seq, dq), BF16),
        compiler_params=_cparams(("parallel", "parallel"), 48),
        name="swa_prompt",
    )(sinks, qkv, qkv, qkv, bias_swa)


def _swa_sample_kernel(q_ref, k_ref, v_ref, bias_ref, sink_ref, o_ref, *, scale):
    for h in range(N_KV_C):
        q = (q_ref[0, h] * scale).astype(BF16)
        s = _dot_nt(q, k_ref[0, h].astype(BF16)) + bias_ref[h]
        sink = sink_ref[h][:, 0:1]
        m = jnp.maximum(jnp.max(s, axis=-1, keepdims=True), sink)
        p = jnp.exp(s - m)
        den = jnp.sum(p, axis=-1, keepdims=True) + jnp.exp(sink - m)
        o_ref[0, h] = (_dot(p.astype(BF16), v_ref[0, h].astype(BF16)) / den).astype(o_ref.dtype)


def swa_sample(q_r, k_r, v_r, bias, sink_rows):
    nb, nkv, ngt, d = q_r.shape
    keys = k_r.shape[2]
    kern = functools.partial(_swa_sample_kernel, scale=HEAD_DIM_C ** -0.5)
    return pl.pallas_call(
        kern,
        grid=(nb,),
        in_specs=[pl.BlockSpec((1, nkv, ngt, d), lambda b: (b, 0, 0, 0)),
                  pl.BlockSpec((1, nkv, keys, d), lambda b: (b, 0, 0, 0)),
                  pl.BlockSpec((1, nkv, keys, d), lambda b: (b, 0, 0, 0)),
                  pl.BlockSpec((nkv, ngt, keys), lambda b: (0, 0, 0)),
                  pl.BlockSpec((nkv, ngt, LANES), lambda b: (0, 0, 0))],
        out_specs=pl.BlockSpec((1, nkv, ngt, d), lambda b: (b, 0, 0, 0)),
        out_shape=jax.ShapeDtypeStruct((nb, nkv, ngt, d), BF16),
        compiler_params=_cparams(("parallel",), 32),
        name="swa_sample",
    )(q_r, k_r, v_r, bias, sink_rows)


def _lower_bound(lbp_ref, layer):
    x = lbp_ref[...]
    e = jnp.exp(x - jnp.max(x, axis=0, keepdims=True))
    den = jnp.sum(e, axis=0, keepdims=True)
    num = e[1:2]
    for l in range(2, layer + 1):
        num = num + e[l:l + 1]
    if layer == 0:
        num = jnp.zeros_like(den)
    return num / den


def _cumsum_rows(x):
    rows = x.shape[0]
    row = lax.broadcasted_iota(jnp.int32, x.shape, 0)
    sh = 1
    while sh < rows:
        x = x + jnp.where(row >= sh, pltpu.roll(x, sh, axis=0), 0.0)
        sh *= 2
    return x


def _gla_gates(qr, fr, lb):
    q = qr * _sigmoid(qr)
    f = lb + (1.0 - lb) * _sigmoid(fr)
    return q, 1.0 - f, jnp.log(f)


def _gla_scores(q, k, bcum, sub):
    rows = q.shape[0]
    sl = 8
    srow = lax.broadcasted_iota(jnp.int32, (sl, LANES), 0)
    lane = lax.broadcasted_iota(jnp.int32, (sl, LANES), 1)
    out = []
    for i in range(rows // sub):
        lo = i * sub
        qi, ki, bi = q[lo:lo + sub], k[lo:lo + sub], bcum[lo:lo + sub]
        a = jnp.zeros((sub, LANES), F32)
        if i > 0:
            ref = bcum[lo - 1:lo]
            qd = qi * jnp.exp(bi - ref)
            kd = jnp.concatenate([k[:lo] * jnp.exp(ref - bcum[:lo]), jnp.zeros((rows - lo, LANES), F32)], axis=0)
            a = _dot_nt(qd.astype(BF16), kd.astype(BF16))
            if rows < LANES:
                a = jnp.concatenate([a, jnp.zeros((sub, LANES - rows), F32)], axis=1)
        slabs = [a[r:r + sl] for r in range(0, sub, sl)]
        for s in range(sub):
            for n in range(s // sl, sub // sl):
                r = n * sl
                d = bi[r:r + sl] - bi[s:s + 1]
                if n == s // sl:
                    d = jnp.where(srow + r >= s, d, -jnp.inf)
                col = jnp.sum(qi[r:r + sl] * jnp.exp(d) * ki[s:s + 1], axis=-1, keepdims=True)
                slabs[n] = jnp.where(lane == lo + s, col, slabs[n])
        out.extend(slabs)
    return out[0] if len(out) == 1 else jnp.concatenate(out, axis=0)


def _pad_rows(x, rows):
    if x.shape[0] == rows:
        return x
    return jnp.concatenate([x, jnp.zeros((rows - x.shape[0], x.shape[1]), x.dtype)], axis=0)


def _gla_chunk(st, q, k, v, g, sub, last):
    bcum = _cumsum_rows(g)
    b_last = bcum[last:last + 1]
    o = _dot_nt((q * jnp.exp(bcum)).astype(BF16), st.astype(BF16))
    a = _gla_scores(q, k, bcum, sub)
    v128 = _pad_rows(v, LANES).astype(BF16)
    o = o + _dot(a.astype(BF16), v128)
    kdec = _pad_rows(k * jnp.exp(b_last - bcum), LANES).astype(BF16)
    vt = _pad_rows(v, LANES).T.astype(BF16)
    st_new = st * jnp.exp(b_last) + _dot(vt, kdec)
    return o, st_new


def _gla_finish(o, gate, ng_ref):
    return (_rms(o, ng_ref[...]) * (gate * _sigmoid(gate))).astype(BF16)


def _hgrn_prompt_kernel(lbp_ref, hq_ref, hf_ref, hi_ref, hg_ref, ng_ref, y_ref, sfin_ref, st_scr, *, layer):
    r = pl.program_id(2)
    c = CHUNK_B

    @pl.when(r == 0)
    def _():
        st_scr[...] = jnp.zeros(st_scr.shape, F32)

    lb = _lower_bound(lbp_ref, layer)

    def body(ci, carry):
        r0 = pl.multiple_of(ci * c, c)
        rows = pl.ds(r0, c)
        q, k, g = _gla_gates(hq_ref[rows, :], hf_ref[rows, :], lb)
        o, st_new = _gla_chunk(st_scr[...], q, k, hi_ref[rows, :], g, SUB_B, c - 1)
        st_scr[...] = st_new
        y_ref[rows, :] = _gla_finish(o, hg_ref[rows, :], ng_ref)
        return carry

    lax.fori_loop(0, hq_ref.shape[0] // c, body, 0)

    @pl.when(r == pl.num_programs(2) - 1)
    def _():
        sfin_ref[0, 0] = st_scr[...].T


def hgrn_prompt(h_in, nb, seq, lb_param, norm_g, layer):
    nh = h_in.shape[1] // (4 * LANES)
    rows = min(ROWS_B, seq)
    nr = seq // rows
    depth = lb_param.shape[0]
    kern = functools.partial(_hgrn_prompt_kernel, layer=layer)

    def col(j):
        return pl.BlockSpec((rows, LANES), lambda b, h, r: (b * nr + r, j * nh + h))

    return pl.pallas_call(
        kern,
        grid=(nb, nh, nr),
        in_specs=[pl.BlockSpec((depth, LANES), lambda b, h, r: (0, h)),
                  col(0), col(1), col(2), col(3),
                  pl.BlockSpec((1, LANES), lambda b, h, r: (0, 0))],
        out_specs=[pl.BlockSpec((rows, LANES), lambda b, h, r: (b * nr + r, h)),
                   pl.BlockSpec((1, 1, LANES, LANES), lambda b, h, r: (b, h, 0, 0))],
        out_shape=[jax.ShapeDtypeStruct((nb * seq, nh * LANES), BF16),
                   jax.ShapeDtypeStruct((nb, nh, LANES, LANES), F32)],
        scratch_shapes=[pltpu.VMEM((LANES, LANES), F32)],
        compiler_params=_cparams(("parallel", "parallel", "arbitrary"), 32),
        name="hgrn_prompt",
    )(lb_param, h_in, h_in, h_in, h_in, norm_g.reshape(1, LANES))


def _hgrn_sample_kernel(lbp_ref, h_ref, ng_ref, s0_ref, y_ref, s1_ref, *, layer, tdec):
    nh = s0_ref.shape[1]
    rows = h_ref.shape[1]
    for hd in range(nh):
        def cols(j, hd=hd):
            return slice((j * nh + hd) * LANES, (j * nh + hd + 1) * LANES)
        lb = _lower_bound(lbp_ref.at[:, cols(0)], layer)
        q, k, g = _gla_gates(h_ref[0, :, cols(0)], h_ref[0, :, cols(1)], lb)
        o, st_new = _gla_chunk(s0_ref[0, hd].T, q, k, h_ref[0, :, cols(2)], g, rows, tdec - 1)
        y_ref[0, :, cols(0)] = _gla_finish(o, h_ref[0, :, cols(3)], ng_ref)
        s1_ref[0, hd] = st_new.T


def hgrn_sample(h_in, state, state_layer, lb_param, norm_g, layer, tdec):
    nb, rows, width = h_in.shape
    nh = width // (4 * LANES)
    depth = lb_param.shape[0]
    kern = functools.partial(_hgrn_sample_kernel, layer=layer, tdec=tdec)
    return pl.pallas_call(
        kern,
        grid=(nb,),
        in_specs=[pl.BlockSpec((depth, nh * LANES), lambda b: (0, 0)),
                  pl.BlockSpec((1, rows, width), lambda b: (b, 0, 0)),
                  pl.BlockSpec((1, LANES), lambda b: (0, 0)),
                  pl.BlockSpec((None, 1, nh, LANES, LANES), lambda b: (state_layer, b, 0, 0, 0))],
        out_specs=[pl.BlockSpec((1, rows, nh * LANES), lambda b: (b, 0, 0)),
                   pl.BlockSpec((1, nh, LANES, LANES), lambda b: (b, 0, 0, 0))],
        out_shape=[jax.ShapeDtypeStruct((nb, rows, nh * LANES), BF16),
                   jax.ShapeDtypeStruct((nb, nh, LANES, LANES), F32)],
        compiler_params=_cparams(("parallel",), 32),
        name="hgrn_sample",
    )(lb_param, h_in, norm_g.reshape(1, LANES), state)


def _row_tile(m, cands=(640, 512, 256, 128, 64, 32, 16, 8)):
    for t in cands:
        if m % t == 0:
            return t
    raise ValueError(f"unsupported row count {m}")


def _col_tile(n, cands):
    for t in cands:
        if n % t == 0:
            return t
    raise ValueError(f"unsupported column count {n}")


def kernel(x_prompt, x_sample, cache_kv_a, state_hgrn_b, cache_win_c, page_table, norm_g, a_w_in, a_w_out, a_lambda, a_subln_g, b_w_in, b_w_out, b_lower_bound, b_norm_g, c_w_in, c_w_out, c_sinks, rel_bias_table, ffn_w_gate_up, ffn_w_down):
    nb, seq, d = x_prompt.shape
    nbs, tdec, _ = x_sample.shape
    depth = norm_g.shape[0]
    mp, ms = nb * seq, nbs * tdec
    m = mp + ms
    tm = _row_tile(m)
    tm_in = _row_tile(m, (1040, 640, 512, 256, 128, 64, 32, 16, 8))
    page = cache_kv_a.shape[2]
    assert page == LANES and page >= MAX_DISTANCE and tdec <= 8
    kvw = N_KV_A * 2 * HEAD_DIM_A
    cache = cache_kv_a.reshape(cache_kv_a.shape[0], cache_kv_a.shape[1], page * 2 * N_KV_A, 2 * HEAD_DIM_A)
    table = rel_bias_table.astype(F32)
    ga, gc = GROUP_A, GROUP_C
    wb = cache_win_c.shape[2]
    keys_c = 2 * WINDOW

    dist_a = _band_dist(TQ_A)
    bias_a = bias_tiles(table, dist_a, dist_a >= 0)
    dist_c = _band_dist(WINDOW)
    bias_c = bias_tiles(table, dist_c, (dist_c >= 0) & (dist_c <= WINDOW))

    dist_as, in_as = _decode_dist(tdec, page, 2 * LANES)
    bias_as = bias_tiles(table, dist_as, in_as & (dist_as >= 0))
    bias_as = bias_as.reshape(N_KV_A, ga, 2, 8, 2 * LANES).transpose(0, 2, 1, 3, 4)[:, :, :, :tdec]
    bias_as = bias_as.reshape(N_KV_A, 2 * ga * tdec, 2 * LANES)
    last_a, new_a = bias_as[:, :, :LANES], bias_as[:, :, LANES:]
    far_a = table[NUM_BUCKETS - 1].reshape(N_KV_A, ga, 2).transpose(0, 2, 1)
    far_a = jnp.broadcast_to(far_a[:, :, :, None, None], (N_KV_A, 2, ga, tdec, LANES))
    far_a = far_a.reshape(N_KV_A, 2 * ga * tdec, LANES)
    dist_cs, in_cs = _decode_dist(tdec, wb, keys_c)
    bias_cs = bias_tiles(table, dist_cs, in_cs & (dist_cs >= 0) & (dist_cs <= WINDOW))
    bias_cs = bias_cs.reshape(N_KV_C, gc, 8, keys_c)[:, :, :tdec].reshape(N_KV_C, gc * tdec, keys_c)

    a_w_in16, a_w_out16 = a_w_in.astype(BF16), a_w_out.astype(BF16)
    b_w_in16, b_w_out16 = b_w_in.astype(BF16), b_w_out.astype(BF16)
    c_w_in16, c_w_out16 = c_w_in.astype(BF16), c_w_out.astype(BF16)
    ffn_w_down16 = ffn_w_down.astype(BF16)
    dff = ffn_w_gate_up.shape[2] // 2

    h = jnp.concatenate([x_prompt.reshape(mp, d), x_sample.reshape(ms, d)], axis=0)
    kv_p, kv_s, hg_p, hg_s, win_p, win_s = [], [], [], [], [], []
    for layer in range(depth):
        kind, j = layer % 3, layer // 3
        g = norm_g[layer]
        if kind == 0:
            lam_init = 0.8 - 0.6 * math.exp(-0.3 * layer)
            nslot = 2 * kvw // LANES
            qkv, kv_rows = norm_matmul(h, g[0], a_w_in16, j, tm=tm_in, tn=2 * kvw, tail_rows=True)
            dq = N_KV_A * GROUP_A * 2 * HEAD_DIM_A
            lam_q, lam_k = a_lambda[j][0::2], a_lambda[j][1::2]
            yp = attn_a_prompt(qkv, nb, seq, table, bias_a, lam_q, lam_k, a_subln_g[j], lam_init)
            qs = qkv[mp:, :dq].reshape(nbs, tdec, N_KV_A, ga, LANES).transpose(0, 2, 3, 1, 4)
            qs = qs.reshape(nbs, N_KV_A, ga * tdec, LANES)
            kvs = qkv[mp:, dq:].reshape(nbs, tdec, 2 * kvw)
            ys = attn_a_sample(qs, cache, j, page_table, kvs, far_a, last_a, new_a,
                               lam_q, lam_k, a_subln_g[j], lam_init)
            ys = ys.reshape(nbs, N_KV_A, ga, tdec, LANES).transpose(0, 3, 1, 2, 4).reshape(ms, dq)
            kv_p.append(kv_rows[:mp * nslot].reshape(nb, seq, 2, N_KV_A, 2 * HEAD_DIM_A))
            kv_s.append(kv_rows[mp * nslot:].reshape(nbs, tdec, 2, N_KV_A, 2 * HEAD_DIM_A))
            w_out = a_w_out16
        elif kind == 1:
            hin = norm_matmul(h, g[0], b_w_in16, j, tm=tm_in, tn=_col_tile(b_w_in.shape[2], (1024, 512, 256, 128)))
            yp, sp = hgrn_prompt(hin, nb, seq, b_lower_bound, b_norm_g[j], layer)
            hs_in = jnp.pad(hin[mp:].reshape(nbs, tdec, hin.shape[1]), ((0, 0), (0, 8 - tdec), (0, 0)))
            ys, ss = hgrn_sample(hs_in, state_hgrn_b, j, b_lower_bound, b_norm_g[j], layer, tdec)
            ys = ys[:, :tdec].reshape(ms, d)
            hg_p.append(sp)
            hg_s.append(ss)
            w_out = b_w_out16
        else:
            qkv = norm_matmul(h, g[0], c_w_in16, j, tm=tm_in, tn=_col_tile(c_w_in.shape[2], (1280, 512, 256, 128)))
            dq = N_KV_C * GROUP_C * HEAD_DIM_C
            dkv = N_KV_C * HEAD_DIM_C
            yp = swa_prompt(qkv, nb, seq, c_sinks[j], bias_c)
            kv_new = qkv[mp:, dq:].reshape(nbs, tdec, 2 * dkv)
            kv_all = jnp.concatenate([cache_win_c[j].reshape(nbs, wb, 2 * dkv), kv_new], axis=1)
            kv_pad = jnp.pad(kv_all, ((0, 0), (0, keys_c - wb - tdec), (0, 0)))
            kv_r = kv_pad.reshape(nbs, keys_c, 2, N_KV_C, HEAD_DIM_C).transpose(2, 0, 3, 1, 4)
            qs = qkv[mp:, :dq].reshape(nbs, tdec, N_KV_C, gc, HEAD_DIM_C).transpose(0, 2, 3, 1, 4)
            qs = qs.reshape(nbs, N_KV_C, gc * tdec, HEAD_DIM_C)
            sink_rows = jnp.broadcast_to(c_sinks[j].reshape(N_KV_C, gc, 1, 1), (N_KV_C, gc, tdec, LANES))
            sink_rows = sink_rows.reshape(N_KV_C, gc * tdec, LANES)
            ys = swa_sample(qs, kv_r[0], kv_r[1], bias_cs, sink_rows)
            ys = ys.reshape(nbs, N_KV_C, gc, tdec, HEAD_DIM_C).transpose(0, 3, 1, 2, 4).reshape(ms, dq)
            win_p.append(qkv[:mp, dq:].reshape(nb, seq, 2, N_KV_C, HEAD_DIM_C)[:, seq - wb:])
            win_s.append(kv_all[:, tdec:].reshape(nbs, wb, 2, N_KV_C, HEAD_DIM_C))
            w_out = c_w_out16
        y = jnp.concatenate([yp, ys], axis=0)
        h = matmul_postnorm_residual(y, w_out, j, g[1], h, tm=tm)
        hm = norm_swiglu(h, g[2], ffn_w_gate_up, layer, tm=tm_in, tn=_col_tile(dff, (512, 256, 128)))
        h = matmul_postnorm_residual(hm, ffn_w_down16, layer, g[3], h, tm=tm)
    return (h[:mp].reshape(nb, seq, d), h[mp:].reshape(nbs, tdec, d),
            jnp.stack(kv_p), jnp.stack(kv_s), jnp.stack(hg_p), jnp.stack(hg_s),
            jnp.stack(win_p), jnp.stack(win_s))
```

```python
import functools
import math

import numpy as np
import jax
import jax.numpy as jnp
from jax import lax
from jax.experimental import pallas as pl
from jax.experimental.pallas import tpu as pltpu

F32 = jnp.float32
BF16 = jnp.bfloat16

NORM_EPS = 1e-6
NUM_BUCKETS = 32
MAX_DISTANCE = 128
NEG = -1e30
LANES = 128
HALF = LANES // 2
MIB = 1024 * 1024

HEAD_DIM_A = 64
N_KV_A = 4
GROUP_A = 4
HEAD_DIM_C = 64
N_KV_C = 4
GROUP_C = 8
WINDOW = 128
TQ_A = 256
PAGES_PER_STEP = 32
PAGE_GROUP = 32
CHUNK_B = 128
SUB_B = 16
ROWS_B = 512
HEADS_B = 4


def _cparams(sem, vmem_mib):
    return pltpu.CompilerParams(dimension_semantics=sem, vmem_limit_bytes=vmem_mib * MIB)


def _rms(x, g):
    ms = jnp.mean(x * x, axis=-1, keepdims=True)
    return x * lax.rsqrt(ms + NORM_EPS) * g


def _sigmoid(x):
    return 1.0 / (1.0 + jnp.exp(-x))


def _dot(a, b):
    return jnp.dot(a, b, preferred_element_type=F32)


def _dot_nt(a, b):
    return lax.dot_general(a, b, (((1,), (1,)), ((), ())), preferred_element_type=F32)


def _norm_matmul_kernel(x_ref, g_ref, w_ref, o_ref, *rest, n_tail):
    xn_ref = rest[-1]
    j = pl.program_id(1)

    @pl.when(j == 0)
    def _():
        xn_ref[...] = _rms(x_ref[...], g_ref[...]).astype(BF16)

    res = _dot(xn_ref[...], w_ref[...])
    o_ref[...] = res
    if n_tail:
        tail_ref = rest[0]
        tm = o_ref.shape[0]

        @pl.when(j == pl.num_programs(1) - 1)
        def _():
            for slot in range(n_tail):
                tail_ref[pl.ds(slot, tm, stride=n_tail), :] = res[:, slot * LANES:(slot + 1) * LANES]


def norm_matmul(x, g, w, layer, *, tm, tn, tail_rows=False):
    m, k = x.shape
    n = w.shape[2]
    assert m % tm == 0 and n % tn == 0
    n_tail = tn // LANES if tail_rows else 0
    out_specs = [pl.BlockSpec((tm, tn), lambda i, j: (i, j))]
    out_shape = [jax.ShapeDtypeStruct((m, n), F32)]
    if tail_rows:
        out_specs.append(pl.BlockSpec((tm * n_tail, LANES), lambda i, j: (i, 0)))
        out_shape.append(jax.ShapeDtypeStruct((m * n_tail, LANES), F32))
    out = pl.pallas_call(
        functools.partial(_norm_matmul_kernel, n_tail=n_tail),
        grid=(m // tm, n // tn),
        in_specs=[pl.BlockSpec((tm, k), lambda i, j: (i, 0)),
                  pl.BlockSpec((1, k), lambda i, j: (0, 0)),
                  pl.BlockSpec((None, k, tn), lambda i, j: (layer, 0, j))],
        out_specs=out_specs,
        out_shape=out_shape,
        scratch_shapes=[pltpu.VMEM((tm, k), BF16)],
        compiler_params=_cparams(("parallel", "arbitrary"), 56),
        name="norm_matmul",
    )(x, g.reshape(1, k), w)
    return out if tail_rows else out[0]


def _norm_swiglu_kernel(x_ref, g_ref, wg_ref, wu_ref, o_ref, xn_ref):
    @pl.when(pl.program_id(1) == 0)
    def _():
        xn_ref[...] = _rms(x_ref[...], g_ref[...]).astype(BF16)

    xn = xn_ref[...]
    gate = _dot(xn, wg_ref[...].astype(BF16))
    up = _dot(xn, wu_ref[...].astype(BF16))
    o_ref[...] = (gate * _sigmoid(gate) * up).astype(o_ref.dtype)


def norm_swiglu(x, g, w_gate_up, layer, *, tm, tn):
    m, k = x.shape
    dff = w_gate_up.shape[2] // 2
    assert m % tm == 0 and dff % tn == 0
    nj = dff // tn
    return pl.pallas_call(
        _norm_swiglu_kernel,
        grid=(m // tm, nj),
        in_specs=[pl.BlockSpec((tm, k), lambda i, j: (i, 0)),
                  pl.BlockSpec((1, k), lambda i, j: (0, 0)),
                  pl.BlockSpec((None, k, tn), lambda i, j: (layer, 0, j)),
                  pl.BlockSpec((None, k, tn), lambda i, j: (layer, 0, j + nj))],
        out_specs=pl.BlockSpec((tm, tn), lambda i, j: (i, j)),
        out_shape=jax.ShapeDtypeStruct((m, dff), BF16),
        scratch_shapes=[pltpu.VMEM((tm, k), BF16)],
        compiler_params=_cparams(("parallel", "arbitrary"), 56),
        name="norm_swiglu",
    )(x, g.reshape(1, k), w_gate_up, w_gate_up)


def _matmul_postnorm_kernel(y_ref, w_ref, g_ref, h_ref, o_ref, acc_ref, *, nk):
    def finish(acc):
        o_ref[...] = h_ref[...] + _rms(acc, g_ref[...])

    if nk == 1:
        finish(_dot(y_ref[...], w_ref[...]))
        return
    k = pl.program_id(1)

    @pl.when(k == 0)
    def _():
        acc_ref[...] = _dot(y_ref[...], w_ref[...])

    @pl.when(jnp.logical_and(k > 0, k < nk - 1))
    def _():
        acc_ref[...] += _dot(y_ref[...], w_ref[...])

    @pl.when(k == nk - 1)
    def _():
        finish(acc_ref[...] + _dot(y_ref[...], w_ref[...]))


def matmul_postnorm_residual(y, w, layer, g, h, *, tm):
    m, kdim = y.shape
    n = w.shape[2]
    tk = kdim if kdim <= 2048 else _col_tile(kdim, (1408, 1024, 512, 256, 128))
    assert m % tm == 0 and kdim % tk == 0
    nk = kdim // tk
    return pl.pallas_call(
        functools.partial(_matmul_postnorm_kernel, nk=nk),
        grid=(m // tm, nk),
        in_specs=[pl.BlockSpec((tm, tk), lambda i, k: (i, k)),
                  pl.BlockSpec((None, tk, n), lambda i, k: (layer, k, 0)),
                  pl.BlockSpec((1, n), lambda i, k: (0, 0)),
                  pl.BlockSpec((tm, n), lambda i, k: (i, 0))],
        out_specs=pl.BlockSpec((tm, n), lambda i, k: (i, 0)),
        out_shape=jax.ShapeDtypeStruct((m, n), F32),
        scratch_shapes=[pltpu.VMEM((tm, n) if nk > 1 else (8, LANES), F32)],
        compiler_params=_cparams(("parallel", "arbitrary"), 56),
        name="matmul_postnorm_residual",
    )(y, w, g.reshape(1, n), h)


def _bucket_np(dist):
    n = np.maximum(dist, 0)
    max_exact = NUM_BUCKETS // 2
    ratio = np.log(np.maximum(n, 1).astype(np.float32) / np.float32(max_exact)) / np.float32(
        math.log(MAX_DISTANCE / max_exact))
    large = np.minimum(max_exact + (ratio * (NUM_BUCKETS - max_exact)).astype(np.int32), NUM_BUCKETS - 1)
    return np.where(n < max_exact, n, large).astype(np.int32)


def _bias_tile_kernel(tab_ref, bucket_ref, mask_ref, o_ref):
    c = pl.program_id(0)
    b = bucket_ref[...]
    acc = jnp.zeros(b.shape, F32)
    for k in range(NUM_BUCKETS):
        acc = jnp.where(b == k, tab_ref[k, c], acc)
    o_ref[0] = jnp.where(mask_ref[...] > 0, acc, NEG)


def bias_tiles(table, dist, valid):
    ncol = table.shape[1]
    r, c = dist.shape
    return pl.pallas_call(
        _bias_tile_kernel,
        grid=(ncol,),
        in_specs=[pl.BlockSpec(memory_space=pltpu.SMEM),
                  pl.BlockSpec((r, c), lambda i: (0, 0)),
                  pl.BlockSpec((r, c), lambda i: (0, 0))],
        out_specs=pl.BlockSpec((1, r, c), lambda i: (i, 0, 0)),
        out_shape=jax.ShapeDtypeStruct((ncol, r, c), F32),
        compiler_params=_cparams(("parallel",), 32),
        name="bias_tiles",
    )(table, jnp.asarray(_bucket_np(dist)), jnp.asarray(valid.astype(np.int32)))


def _band_dist(t):
    return np.arange(t)[:, None] + t - np.arange(2 * t)[None, :]


def _decode_dist(tdec, n_past, cols):
    t = np.arange(8)[:, None]
    k = np.arange(cols)[None, :]
    dist = np.where(k < n_past, t + n_past - k, t - (k - n_past))
    inside = (k < n_past + tdec) & (t < tdec)
    return dist, inside


def _diff_lambda(lq_ref, lk_ref, lam_init):
    e = jnp.exp(jnp.sum(lq_ref[...] * lk_ref[...], axis=-1, keepdims=True))
    return e[0:1] - e[1:2] + lam_init


def _split_maps(q):
    lane = lax.broadcasted_iota(jnp.int32, q.shape, 1)
    return jnp.where(lane < HALF, q, 0.0), jnp.where(lane >= HALF, q, 0.0)


def _attn_a_prompt_kernel(tab_ref, lq_ref, lk_ref, q_ref, k_ref, v_ref, bias_ref, sg_ref, o_ref,
                          k_scr, v_scr, s_scr, m_scr, l_scr, acc_scr, *, lam_init, scale):
    kvh = pl.program_id(0)
    qi = pl.program_id(2)
    tq = q_ref.shape[0]
    nt = k_scr.shape[0]
    ng = GROUP_A
    nrg = 2 * ng

    @pl.when(qi == 0)
    def _():
        for t in range(nt):
            k_scr[t] = k_ref[t * tq:(t + 1) * tq, :].astype(BF16)
            v_scr[t] = v_ref[t * tq:(t + 1) * tq, :].astype(BF16)

    q = q_ref[...] * scale
    parts1, parts2 = [], []
    for g in range(ng):
        q1, q2 = _split_maps(q[:, g * LANES:(g + 1) * LANES])
        parts1.append(q1)
        parts2.append(q2)
    qs = jnp.concatenate(parts1 + parts2, axis=0).astype(BF16)

    def score_tile(t, bias_of_group, first=False):
        s = _dot_nt(qs, k_scr[t])
        for rg in range(nrg):
            rows = slice(rg * tq, (rg + 1) * tq)
            sg = s[rows] + bias_of_group(rg)
            s_scr[t, rows, :] = sg
            mx = sg[:, :LANES]
            for c in range(1, tq // LANES):
                mx = jnp.maximum(mx, sg[:, c * LANES:(c + 1) * LANES])
            m_scr[rows, :] = mx if first else jnp.maximum(m_scr[rows, :], mx)

    def col_of_group(rg):
        m, g = divmod(rg, ng)
        return 2 * g + m

    score_tile(qi, lambda rg: bias_ref[col_of_group(rg), :, tq:2 * tq], first=True)

    def far_body(t, carry):
        score_tile(t, lambda rg: tab_ref[NUM_BUCKETS - 1, kvh * nrg + col_of_group(rg)])
        return carry

    lax.fori_loop(0, qi - 1, far_body, 0)

    @pl.when(qi >= 1)
    def _():
        score_tile(qi - 1, lambda rg: bias_ref[col_of_group(rg), :, 0:tq])

    m_row = jnp.max(m_scr[...], axis=-1, keepdims=True)

    def pv_tile(t, first=False):
        p = jnp.exp(s_scr[t] - m_row)
        ps = p[:, :LANES]
        for c in range(1, tq // LANES):
            ps = ps + p[:, c * LANES:(c + 1) * LANES]
        pv = _dot(p.astype(BF16), v_scr[t])
        if first:
            l_scr[...] = ps
            acc_scr[...] = pv
        else:
            l_scr[...] += ps
            acc_scr[...] += pv

    pv_tile(qi, first=True)

    def pv_body(t, carry):
        pv_tile(t)
        return carry

    lax.fori_loop(0, qi, pv_body, 0)

    l_row = jnp.sum(l_scr[...], axis=-1, keepdims=True)
    o_all = acc_scr[...] / l_row
    half = ng * tq
    lam = _diff_lambda(lq_ref, lk_ref, lam_init)
    o = o_all[:half] - lam * o_all[half:]
    for g in range(ng):
        og = _rms(o[g * tq:(g + 1) * tq], sg_ref[...]) * (1.0 - lam_init)
        o_ref[:, g * LANES:(g + 1) * LANES] = og.astype(o_ref.dtype)


def attn_a_prompt(qkv, nb, seq, table, bias_near, lam_q, lam_k, subln_g, lam_init):
    tq = TQ_A
    nq = seq // tq
    nqb = (N_KV_A * GROUP_A * LANES) // LANES
    kern = functools.partial(_attn_a_prompt_kernel, lam_init=lam_init, scale=HEAD_DIM_A ** -0.5)
    nrow = 2 * GROUP_A * tq
    return pl.pallas_call(
        kern,
        grid=(N_KV_A, nb, nq),
        in_specs=[pl.BlockSpec(memory_space=pltpu.SMEM),
                  pl.BlockSpec((2, HEAD_DIM_A), lambda h, b, i: (0, 0)),
                  pl.BlockSpec((2, HEAD_DIM_A), lambda h, b, i: (0, 0)),
                  pl.BlockSpec((tq, GROUP_A * LANES), lambda h, b, i: (b * nq + i, h)),
                  pl.BlockSpec((seq, LANES), lambda h, b, i: (b, nqb + h)),
                  pl.BlockSpec((seq, LANES), lambda h, b, i: (b, nqb + N_KV_A + h)),
                  pl.BlockSpec((2 * GROUP_A, tq, 2 * tq), lambda h, b, i: (h, 0, 0)),
                  pl.BlockSpec((1, LANES), lambda h, b, i: (0, 0))],
        out_specs=pl.BlockSpec((tq, GROUP_A * LANES), lambda h, b, i: (b * nq + i, h)),
        out_shape=jax.ShapeDtypeStruct((nb * seq, N_KV_A * GROUP_A * LANES), BF16),
        scratch_shapes=[pltpu.VMEM((nq, tq, LANES), BF16),
                        pltpu.VMEM((nq, tq, LANES), BF16),
                        pltpu.VMEM((nq, nrow, tq), F32),
                        pltpu.VMEM((nrow, LANES), F32),
                        pltpu.VMEM((nrow, LANES), F32),
                        pltpu.VMEM((nrow, LANES), F32)],
        compiler_params=_cparams(("parallel", "parallel", "arbitrary"), 56),
        name="attn_a_prompt",
    )(table, lam_q, lam_k, qkv, qkv, qkv, bias_near, subln_g.reshape(1, LANES))


def _attn_a_sample_kernel(pt_ref, lq_ref, lk_ref, q_ref, *rest, lam_init, scale, npp):
    page_refs = rest[:npp]
    (knew_ref, bfar_ref, blast_ref, bnew_ref, sg_ref, o_ref, m_scr, l_scr, acc_scr) = rest[npp:]
    ps = pl.program_id(1)
    nsteps = pl.num_programs(1)
    nkv = N_KV_A
    tdec = knew_ref.shape[1]

    @pl.when(ps == 0)
    def _():
        m_scr[...] = jnp.full(m_scr.shape, -jnp.inf, F32)
        l_scr[...] = jnp.zeros(l_scr.shape, F32)
        acc_scr[...] = jnp.zeros(acc_scr.shape, F32)

    qs32 = []
    for h in range(nkv):
        q1, q2 = _split_maps(q_ref[0, h] * scale)
        qs32.append(jnp.concatenate([q1, q2], axis=0))
    qs = [x.astype(BF16) for x in qs32]
    is_last = ps == nsteps - 1

    nslot = 2 * nkv
    page = page_refs[0].shape[0] // nslot

    def slot_rows(slot, refs):
        rows = [r[pl.ds(slot, page, stride=nslot), :] for r in refs]
        return jnp.concatenate(rows, axis=0).astype(BF16)

    grp = min(PAGE_GROUP, npp)
    for g0 in range(0, npp, grp):
        refs = page_refs[g0:g0 + grp]
        for h in range(nkv):
            kh = slot_rows(h, refs)
            vh = slot_rows(nkv + h, refs)
            s = _dot_nt(qs[h], kh)
            bfar = bfar_ref[h]
            if g0 + grp == npp:
                tail = jnp.where(is_last, blast_ref[h], bfar)
                s = jnp.concatenate([s[:, :-page] + bfar[:, 0:1], s[:, -page:] + tail], axis=1)
            else:
                s = s + bfar[:, 0:1]
            m_prev = m_scr[h]
            m_new = jnp.maximum(m_prev, jnp.max(s, axis=-1, keepdims=True))
            alpha = jnp.exp(m_prev - m_new)
            p = jnp.exp(s - m_new[:, 0:1])
            l_scr[h] = alpha * l_scr[h] + jnp.sum(p, axis=-1, keepdims=True)
            acc_scr[h] = alpha * acc_scr[h] + _dot(p.astype(BF16), vh)
            m_scr[h] = m_new

    @pl.when(is_last)
    def _():
        lam = _diff_lambda(lq_ref, lk_ref, lam_init)
        knew = knew_ref[0]
        for h in range(nkv):
            kn = knew[:, h * LANES:(h + 1) * LANES]
            vn = knew[:, (nkv + h) * LANES:(nkv + h + 1) * LANES]
            bnew = bnew_ref[h]
            cols = [jnp.sum(qs32[h] * kn[t:t + 1, :], axis=-1, keepdims=True) + bnew[:, t:t + 1]
                    for t in range(tdec)]
            m_prev = m_scr[h]
            m_new = m_prev
            for c in cols:
                m_new = jnp.maximum(m_new, c)
            alpha = jnp.exp(m_prev - m_new)
            l = alpha * l_scr[h]
            acc = alpha * acc_scr[h]
            for t in range(tdec):
                p = jnp.exp(cols[t] - m_new)
                l = l + p
                acc = acc + p * vn[t:t + 1, :]
            o_all = acc / l
            half = o_all.shape[0] // 2
            o = o_all[:half] - lam * o_all[half:]
            o_ref[0, h] = (_rms(o, sg_ref[...]) * (1.0 - lam_init)).astype(o_ref.dtype)


def attn_a_sample(q_r, cache, layer, page_table, knew, bias_far, bias_last, bias_new,
                  lam_q, lam_k, subln_g, lam_init):
    nb, nkv, ngt, _ = q_r.shape
    n_pages = page_table.shape[1]
    page_rows = cache.shape[2]
    assert page_rows == 2 * nkv * LANES and cache.shape[3] == LANES
    npp = min(PAGES_PER_STEP, n_pages)
    assert n_pages % npp == 0
    tdec = knew.shape[1]
    width = knew.shape[2]
    kern = functools.partial(_attn_a_sample_kernel, lam_init=lam_init, scale=HEAD_DIM_A ** -0.5, npp=npp)

    def page_spec(i):
        return pl.BlockSpec((None, None, page_rows, LANES),
                            lambda b, p, pt: (layer, pt[b, p * npp + i], 0, 0))

    const3 = lambda b, p, pt: (0, 0, 0)
    grid_spec = pltpu.PrefetchScalarGridSpec(
        num_scalar_prefetch=1,
        grid=(nb, n_pages // npp),
        in_specs=[pl.BlockSpec((2, HEAD_DIM_A), lambda b, p, pt: (0, 0)),
                  pl.BlockSpec((2, HEAD_DIM_A), lambda b, p, pt: (0, 0)),
                  pl.BlockSpec((1, nkv, ngt, LANES), lambda b, p, pt: (b, 0, 0, 0))]
                 + [page_spec(i) for i in range(npp)]
                 + [pl.BlockSpec((1, tdec, width), lambda b, p, pt: (b, 0, 0)),
                    pl.BlockSpec((nkv, 2 * ngt, LANES), const3),
                    pl.BlockSpec((nkv, 2 * ngt, LANES), const3),
                    pl.BlockSpec((nkv, 2 * ngt, LANES), const3),
                    pl.BlockSpec((1, LANES), lambda b, p, pt: (0, 0))],
        out_specs=pl.BlockSpec((1, nkv, ngt, LANES), lambda b, p, pt: (b, 0, 0, 0)),
        scratch_shapes=[pltpu.VMEM((nkv, 2 * ngt, LANES), F32),
                        pltpu.VMEM((nkv, 2 * ngt, LANES), F32),
                        pltpu.VMEM((nkv, 2 * ngt, LANES), F32)],
    )
    return pl.pallas_call(
        kern,
        grid_spec=grid_spec,
        out_shape=jax.ShapeDtypeStruct((nb, nkv, ngt, LANES), BF16),
        compiler_params=_cparams(("parallel", "arbitrary"), 56),
        name="attn_a_sample",
    )(page_table, lam_q, lam_k, q_r, *([cache] * npp), knew, bias_far, bias_last, bias_new,
      subln_g.reshape(1, LANES))


def _dup_half(x, odd):
    lane = lax.broadcasted_iota(jnp.int32, x.shape, 1)
    rolled = pltpu.roll(x, HALF, axis=1)
    keep = (lane >= HALF) if odd else (lane < HALF)
    return jnp.where(keep, x, rolled)


def _swa_prompt_kernel(sink_ref, q_ref, kvp_ref, kvo_ref, bias_ref, o_ref, *, scale):
    n = pl.program_id(1)
    w = q_ref.shape[0]
    ng = GROUP_C
    kv = jnp.concatenate([kvp_ref[...], kvo_ref[...]], axis=0)
    q = q_ref[...] * scale
    lane = lax.broadcasted_iota(jnp.int32, (w, LANES), 1)
    kcol = lax.broadcasted_iota(jnp.int32, (ng * w, 2 * w), 1)
    first = n == 0
    vpair0 = (N_KV_C * HEAD_DIM_C) // LANES
    for h in range(N_KV_C):
        kd = _dup_half(kv[:, (h // 2) * LANES:(h // 2 + 1) * LANES], h % 2).astype(BF16)
        vd = _dup_half(kv[:, (vpair0 + h // 2) * LANES:(vpair0 + h // 2 + 1) * LANES], h % 2).astype(BF16)
        parts, sinks = [], []
        for g in range(ng):
            qp = q[:, (h * ng // 2 + g // 2) * LANES:(h * ng // 2 + g // 2 + 1) * LANES]
            keep = (lane >= HALF) if g % 2 else (lane < HALF)
            parts.append(jnp.where(keep, qp, 0.0))
            sinks.append(jnp.full((w, 1), sink_ref[h * ng + g], F32))
        qs = jnp.concatenate(parts, axis=0).astype(BF16)
        sink = jnp.concatenate(sinks, axis=0)
        bias = bias_ref[h * ng:(h + 1) * ng].reshape(ng * w, 2 * w)
        bias = jnp.where(jnp.logical_and(first, kcol < w), NEG, bias)
        s = _dot_nt(qs, kd) + bias
        m = jnp.maximum(jnp.max(s, axis=-1, keepdims=True), sink)
        p = jnp.exp(s - m)
        den = jnp.sum(p, axis=-1, keepdims=True) + jnp.exp(sink - m)
        o = _dot(p.astype(BF16), vd) / den
        for j in range(ng // 2):
            oe = o[(2 * j) * w:(2 * j + 1) * w]
            oo = o[(2 * j + 1) * w:(2 * j + 2) * w]
            col = (h * ng // 2 + j) * LANES
            o_ref[:, col:col + LANES] = jnp.where(lane < HALF, oe, oo).astype(o_ref.dtype)


def swa_prompt(qkv, nb, seq, sinks, bias_swa):
    w = WINDOW
    nblk = seq // w
    dq = N_KV_C * GROUP_C * HEAD_DIM_C
    dkv = 2 * N_KV_C * HEAD_DIM_C
    kern = functools.partial(_swa_prompt_kernel, scale=HEAD_DIM_C ** -0.5)
    return pl.pallas_call(
        kern,
        grid=(nb, nblk),
        in_specs=[pl.BlockSpec(memory_space=pltpu.SMEM),
                  pl.BlockSpec((w, dq), lambda b, n: (b * nblk + n, 0)),
                  pl.BlockSpec((w, dkv), lambda b, n: (b * nblk + jnp.maximum(n - 1, 0), dq // dkv)),
                  pl.BlockSpec((w, dkv), lambda b, n: (b * nblk + n, dq // dkv)),
                  pl.BlockSpec((N_KV_C * GROUP_C, w, 2 * w), lambda b, n: (0, 0, 0))],
        out_specs=pl.BlockSpec((w, dq), lambda b, n: (b * nblk + n, 0)),
        out_shape=jax.ShapeDtypeStruct((nb * seq, dq), BF16),
        compiler_params=_cparams(("parallel", "parallel"), 48),
        name="swa_prompt",
    )(sinks, qkv, qkv, qkv, bias_swa)


def _swa_sample_kernel(q_ref, k_ref, v_ref, bias_ref, sink_ref, o_ref, *, scale):
    for h in range(N_KV_C):
        q = (q_ref[0, h] * scale).astype(BF16)
        s = _dot_nt(q, k_ref[0, h].astype(BF16)) + bias_ref[h]
        sink = sink_ref[h][:, 0:1]
        m = jnp.maximum(jnp.max(s, axis=-1, keepdims=True), sink)
        p = jnp.exp(s - m)
        den = jnp.sum(p, axis=-1, keepdims=True) + jnp.exp(sink - m)
        o_ref[0, h] = (_dot(p.astype(BF16), v_ref[0, h].astype(BF16)) / den).astype(o_ref.dtype)


def swa_sample(q_r, k_r, v_r, bias, sink_rows):
    nb, nkv, ngt, d = q_r.shape
    keys = k_r.shape[2]
    kern = functools.partial(_swa_sample_kernel, scale=HEAD_DIM_C ** -0.5)
    return pl.pallas_call(
        kern,
        grid=(nb,),
        in_specs=[pl.BlockSpec((1, nkv, ngt, d), lambda b: (b, 0, 0, 0)),
                  pl.BlockSpec((1, nkv, keys, d), lambda b: (b, 0, 0, 0)),
                  pl.BlockSpec((1, nkv, keys, d), lambda b: (b, 0, 0, 0)),
                  pl.BlockSpec((nkv, ngt, keys), lambda b: (0, 0, 0)),
                  pl.BlockSpec((nkv, ngt, LANES), lambda b: (0, 0, 0))],
        out_specs=pl.BlockSpec((1, nkv, ngt, d), lambda b: (b, 0, 0, 0)),
        out_shape=jax.ShapeDtypeStruct((nb, nkv, ngt, d), BF16),
        compiler_params=_cparams(("parallel",), 32),
        name="swa_sample",
    )(q_r, k_r, v_r, bias, sink_rows)


def _lower_bound(lbp_ref, layer):
    x = lbp_ref[...]
    e = jnp.exp(x - jnp.max(x, axis=0, keepdims=True))
    den = jnp.sum(e, axis=0, keepdims=True)
    num = e[1:2]
    for l in range(2, layer + 1):
        num = num + e[l:l + 1]
    if layer == 0:
        num = jnp.zeros_like(den)
    return num / den


def _cumsum_rows(x):
    rows = x.shape[0]
    row = lax.broadcasted_iota(jnp.int32, x.shape, 0)
    sh = 1
    while sh < rows:
        x = x + jnp.where(row >= sh, pltpu.roll(x, sh, axis=0), 0.0)
        sh *= 2
    return x


def _gla_gates(qr, fr, lb):
    q = qr * _sigmoid(qr)
    f = lb + (1.0 - lb) * _sigmoid(fr)
    return q, 1.0 - f, jnp.log(f)


def _gla_scores(q, k, bcum, sub):
    rows = q.shape[0]
    sl = 8
    srow = lax.broadcasted_iota(jnp.int32, (sl, LANES), 0)
    lane = lax.broadcasted_iota(jnp.int32, (sl, LANES), 1)
    out = []
    for i in range(rows // sub):
        lo = i * sub
        qi, ki, bi = q[lo:lo + sub], k[lo:lo + sub], bcum[lo:lo + sub]
        a = jnp.zeros((sub, LANES), F32)
        if i > 0:
            ref = bcum[lo - 1:lo]
            qd = qi * jnp.exp(bi - ref)
            kd = jnp.concatenate([k[:lo] * jnp.exp(ref - bcum[:lo]), jnp.zeros((rows - lo, LANES), F32)], axis=0)
            a = _dot_nt(qd.astype(BF16), kd.astype(BF16))
            if rows < LANES:
                a = jnp.concatenate([a, jnp.zeros((sub, LANES - rows), F32)], axis=1)
        slabs = [a[r:r + sl] for r in range(0, sub, sl)]
        for s in range(sub):
            for n in range(s // sl, sub // sl):
                r = n * sl
                d = bi[r:r + sl] - bi[s:s + 1]
                if n == s // sl:
                    d = jnp.where(srow + r >= s, d, -jnp.inf)
                col = jnp.sum(qi[r:r + sl] * jnp.exp(d) * ki[s:s + 1], axis=-1, keepdims=True)
                slabs[n] = jnp.where(lane == lo + s, col, slabs[n])
        out.extend(slabs)
    return out[0] if len(out) == 1 else jnp.concatenate(out, axis=0)


def _pad_rows(x, rows):
    if x.shape[0] == rows:
        return x
    return jnp.concatenate([x, jnp.zeros((rows - x.shape[0], x.shape[1]), x.dtype)], axis=0)


def _gla_chunk(st, q, k, v, g, sub, last):
    bcum = _cumsum_rows(g)
    b_last = bcum[last:last + 1]
    o = _dot_nt((q * jnp.exp(bcum)).astype(BF16), st.astype(BF16))
    a = _gla_scores(q, k, bcum, sub)
    v128 = _pad_rows(v, LANES).astype(BF16)
    o = o + _dot(a.astype(BF16), v128)
    kdec = _pad_rows(k * jnp.exp(b_last - bcum), LANES).astype(BF16)
    vt = _pad_rows(v, LANES).T.astype(BF16)
    st_new = st * jnp.exp(b_last) + _dot(vt, kdec)
    return o, st_new


def _gla_finish(o, gate, ng_ref):
    return (_rms(o, ng_ref[...]) * (gate * _sigmoid(gate))).astype(BF16)


def _hgrn_prompt_kernel(lbp_ref, hq_ref, hf_ref, hi_ref, hg_ref, ng_ref, y_ref, sfin_ref, st_scr, *, layer):
    r = pl.program_id(2)
    c = CHUNK_B

    @pl.when(r == 0)
    def _():
        st_scr[...] = jnp.zeros(st_scr.shape, F32)

    nhs = st_scr.shape[0]
    lbs = [_lower_bound(lbp_ref.at[:, hd * LANES:(hd + 1) * LANES], layer) for hd in range(nhs)]

    def body(ci, carry):
        r0 = pl.multiple_of(ci * c, c)
        rows = pl.ds(r0, c)
        for hd in range(nhs):
            cols = slice(hd * LANES, (hd + 1) * LANES)
            q, k, g = _gla_gates(hq_ref[rows, cols], hf_ref[rows, cols], lbs[hd])
            o, st_new = _gla_chunk(st_scr[hd], q, k, hi_ref[rows, cols], g, SUB_B, c - 1)
            st_scr[hd] = st_new
            y_ref[rows, cols] = _gla_finish(o, hg_ref[rows, cols], ng_ref)
        return carry

    lax.fori_loop(0, hq_ref.shape[0] // c, body, 0)

    @pl.when(r == pl.num_programs(2) - 1)
    def _():
        for hd in range(nhs):
            sfin_ref[0, hd] = st_scr[hd].T


def hgrn_prompt(h_in, nb, seq, lb_param, norm_g, layer):
    nh = h_in.shape[1] // (4 * LANES)
    nhs = HEADS_B
    assert nh % nhs == 0
    ng = nh // nhs
    rows = min(ROWS_B, seq)
    nr = seq // rows
    depth = lb_param.shape[0]
    kern = functools.partial(_hgrn_prompt_kernel, layer=layer)

    def col(j):
        return pl.BlockSpec((rows, nhs * LANES), lambda b, h, r: (b * nr + r, j * ng + h))

    return pl.pallas_call(
        kern,
        grid=(nb, ng, nr),
        in_specs=[pl.BlockSpec((depth, nhs * LANES), lambda b, h, r: (0, h)),
                  col(0), col(1), col(2), col(3),
                  pl.BlockSpec((1, LANES), lambda b, h, r: (0, 0))],
        out_specs=[pl.BlockSpec((rows, nhs * LANES), lambda b, h, r: (b * nr + r, h)),
                   pl.BlockSpec((1, nhs, LANES, LANES), lambda b, h, r: (b, h, 0, 0))],
        out_shape=[jax.ShapeDtypeStruct((nb * seq, nh * LANES), BF16),
                   jax.ShapeDtypeStruct((nb, nh, LANES, LANES), F32)],
        scratch_shapes=[pltpu.VMEM((nhs, LANES, LANES), F32)],
        compiler_params=_cparams(("parallel", "parallel", "arbitrary"), 32),
        name="hgrn_prompt",
    )(lb_param, h_in, h_in, h_in, h_in, norm_g.reshape(1, LANES))


def _hgrn_sample_kernel(lbp_ref, h_ref, ng_ref, s0_ref, y_ref, s1_ref, *, layer, tdec):
    nh = s0_ref.shape[1]
    rows = h_ref.shape[1]
    for hd in range(nh):
        def cols(j, hd=hd):
            return slice((j * nh + hd) * LANES, (j * nh + hd + 1) * LANES)
        lb = _lower_bound(lbp_ref.at[:, cols(0)], layer)
        q, k, g = _gla_gates(h_ref[0, :, cols(0)], h_ref[0, :, cols(1)], lb)
        o, st_new = _gla_chunk(s0_ref[0, hd].T, q, k, h_ref[0, :, cols(2)], g, rows, tdec - 1)
        y_ref[0, :, cols(0)] = _gla_finish(o, h_ref[0, :, cols(3)], ng_ref)
        s1_ref[0, hd] = st_new.T


def hgrn_sample(h_in, state, state_layer, lb_param, norm_g, layer, tdec):
    nb, rows, width = h_in.shape
    nh = width // (4 * LANES)
    depth = lb_param.shape[0]
    kern = functools.partial(_hgrn_sample_kernel, layer=layer, tdec=tdec)
    return pl.pallas_call(
        kern,
        grid=(nb,),
        in_specs=[pl.BlockSpec((depth, nh * LANES), lambda b: (0, 0)),
                  pl.BlockSpec((1, rows, width), lambda b: (b, 0, 0)),
                  pl.BlockSpec((1, LANES), lambda b: (0, 0)),
                  pl.BlockSpec((None, 1, nh, LANES, LANES), lambda b: (state_layer, b, 0, 0, 0))],
        out_specs=[pl.BlockSpec((1, rows, nh * LANES), lambda b: (b, 0, 0)),
                   pl.BlockSpec((1, nh, LANES, LANES), lambda b: (b, 0, 0, 0))],
        out_shape=[jax.ShapeDtypeStruct((nb, rows, nh * LANES), BF16),
                   jax.ShapeDtypeStruct((nb, nh, LANES, LANES), F32)],
        compiler_params=_cparams(("parallel",), 32),
        name="hgrn_sample",
    )(lb_param, h_in, norm_g.reshape(1, LANES), state)


def _row_tile(m, cands=(640, 512, 256, 128, 64, 32, 16, 8)):
    for t in cands:
        if m % t == 0:
            return t
    raise ValueError(f"unsupported row count {m}")


def _col_tile(n, cands):
    for t in cands:
        if n % t == 0:
            return t
    raise ValueError(f"unsupported column count {n}")


def kernel(x_prompt, x_sample, cache_kv_a, state_hgrn_b, cache_win_c, page_table, norm_g, a_w_in, a_w_out, a_lambda, a_subln_g, b_w_in, b_w_out, b_lower_bound, b_norm_g, c_w_in, c_w_out, c_sinks, rel_bias_table, ffn_w_gate_up, ffn_w_down):
    nb, seq, d = x_prompt.shape
    nbs, tdec, _ = x_sample.shape
    depth = norm_g.shape[0]
    mp, ms = nb * seq, nbs * tdec
    m = mp + ms
    tm = _row_tile(m)
    tm_in = _row_tile(m, (1040, 640, 512, 256, 128, 64, 32, 16, 8))
    page = cache_kv_a.shape[2]
    assert page == LANES and page >= MAX_DISTANCE and tdec <= 8
    kvw = N_KV_A * 2 * HEAD_DIM_A
    cache = cache_kv_a.reshape(cache_kv_a.shape[0], cache_kv_a.shape[1], page * 2 * N_KV_A, 2 * HEAD_DIM_A)
    table = rel_bias_table.astype(F32)
    ga, gc = GROUP_A, GROUP_C
    wb = cache_win_c.shape[2]
    keys_c = 2 * WINDOW

    dist_a = _band_dist(TQ_A)
    bias_a = bias_tiles(table, dist_a, dist_a >= 0)
    dist_c = _band_dist(WINDOW)
    bias_c = bias_tiles(table, dist_c, (dist_c >= 0) & (dist_c <= WINDOW))

    dist_as, in_as = _decode_dist(tdec, page, 2 * LANES)
    bias_as = bias_tiles(table, dist_as, in_as & (dist_as >= 0))
    bias_as = bias_as.reshape(N_KV_A, ga, 2, 8, 2 * LANES).transpose(0, 2, 1, 3, 4)[:, :, :, :tdec]
    bias_as = bias_as.reshape(N_KV_A, 2 * ga * tdec, 2 * LANES)
    last_a, new_a = bias_as[:, :, :LANES], bias_as[:, :, LANES:]
    far_a = table[NUM_BUCKETS - 1].reshape(N_KV_A, ga, 2).transpose(0, 2, 1)
    far_a = jnp.broadcast_to(far_a[:, :, :, None, None], (N_KV_A, 2, ga, tdec, LANES))
    far_a = far_a.reshape(N_KV_A, 2 * ga * tdec, LANES)
    dist_cs, in_cs = _decode_dist(tdec, wb, keys_c)
    bias_cs = bias_tiles(table, dist_cs, in_cs & (dist_cs >= 0) & (dist_cs <= WINDOW))
    bias_cs = bias_cs.reshape(N_KV_C, gc, 8, keys_c)[:, :, :tdec].reshape(N_KV_C, gc * tdec, keys_c)

    a_w_in16, a_w_out16 = a_w_in.astype(BF16), a_w_out.astype(BF16)
    b_w_in16, b_w_out16 = b_w_in.astype(BF16), b_w_out.astype(BF16)
    c_w_in16, c_w_out16 = c_w_in.astype(BF16), c_w_out.astype(BF16)
    ffn_w_down16 = ffn_w_down.astype(BF16)
    dff = ffn_w_gate_up.shape[2] // 2

    h = jnp.concatenate([x_prompt.reshape(mp, d), x_sample.reshape(ms, d)], axis=0)
    kv_p, kv_s, hg_p, hg_s, win_p, win_s = [], [], [], [], [], []
    for layer in range(depth):
        kind, j = layer % 3, layer // 3
        g = norm_g[layer]
        if kind == 0:
            lam_init = 0.8 - 0.6 * math.exp(-0.3 * layer)
            nslot = 2 * kvw // LANES
            qkv, kv_rows = norm_matmul(h, g[0], a_w_in16, j, tm=tm_in, tn=2 * kvw, tail_rows=True)
            dq = N_KV_A * GROUP_A * 2 * HEAD_DIM_A
            lam_q, lam_k = a_lambda[j][0::2], a_lambda[j][1::2]
            yp = attn_a_prompt(qkv, nb, seq, table, bias_a, lam_q, lam_k, a_subln_g[j], lam_init)
            qs = qkv[mp:, :dq].reshape(nbs, tdec, N_KV_A, ga, LANES).transpose(0, 2, 3, 1, 4)
            qs = qs.reshape(nbs, N_KV_A, ga * tdec, LANES)
            kvs = qkv[mp:, dq:].reshape(nbs, tdec, 2 * kvw)
            ys = attn_a_sample(qs, cache, j, page_table, kvs, far_a, last_a, new_a,
                               lam_q, lam_k, a_subln_g[j], lam_init)
            ys = ys.reshape(nbs, N_KV_A, ga, tdec, LANES).transpose(0, 3, 1, 2, 4).reshape(ms, dq)
            kv_p.append(kv_rows[:mp * nslot].reshape(nb, seq, 2, N_KV_A, 2 * HEAD_DIM_A))
            kv_s.append(kv_rows[mp * nslot:].reshape(nbs, tdec, 2, N_KV_A, 2 * HEAD_DIM_A))
            w_out = a_w_out16
        elif kind == 1:
            hin = norm_matmul(h, g[0], b_w_in16, j, tm=tm_in, tn=_col_tile(b_w_in.shape[2], (1024, 512, 256, 128)))
            yp, sp = hgrn_prompt(hin, nb, seq, b_lower_bound, b_norm_g[j], layer)
            hs_in = jnp.pad(hin[mp:].reshape(nbs, tdec, hin.shape[1]), ((0, 0), (0, 8 - tdec), (0, 0)))
            ys, ss = hgrn_sample(hs_in, state_hgrn_b, j, b_lower_bound, b_norm_g[j], layer, tdec)
            ys = ys[:, :tdec].reshape(ms, d)
            hg_p.append(sp)
            hg_s.append(ss)
            w_out = b_w_out16
        else:
            qkv = norm_matmul(h, g[0], c_w_in16, j, tm=tm_in, tn=_col_tile(c_w_in.shape[2], (1280, 512, 256, 128)))
            dq = N_KV_C * GROUP_C * HEAD_DIM_C
            dkv = N_KV_C * HEAD_DIM_C
            yp = swa_prompt(qkv, nb, seq, c_sinks[j], bias_c)
            kv_new = qkv[mp:, dq:].reshape(nbs, tdec, 2 * dkv)
            kv_all = jnp.concatenate([cache_win_c[j].reshape(nbs, wb, 2 * dkv), kv_new], axis=1)
            kv_pad = jnp.pad(kv_all, ((0, 0), (0, keys_c - wb - tdec), (0, 0)))
            kv_r = kv_pad.reshape(nbs, keys_c, 2, N_KV_C, HEAD_DIM_C).transpose(2, 0, 3, 1, 4)
            qs = qkv[mp:, :dq].reshape(nbs, tdec, N_KV_C, gc, HEAD_DIM_C).transpose(0, 2, 3, 1, 4)
            qs = qs.reshape(nbs, N_KV_C, gc * tdec, HEAD_DIM_C)
            sink_rows = jnp.broadcast_to(c_sinks[j].reshape(N_KV_C, gc, 1, 1), (N_KV_C, gc, tdec, LANES))
            sink_rows = sink_rows.reshape(N_KV_C, gc * tdec, LANES)
            ys = swa_sample(qs, kv_r[0], kv_r[1], bias_cs, sink_rows)
            ys = ys.reshape(nbs, N_KV_C, gc, tdec, HEAD_DIM_C).transpose(0, 3, 1, 2, 4).reshape(ms, dq)
            win_p.append(qkv[:mp, dq:].reshape(nb, seq, 2, N_KV_C, HEAD_DIM_C)[:, seq - wb:])
            win_s.append(kv_all[:, tdec:].reshape(nbs, wb, 2, N_KV_C, HEAD_DIM_C))
            w_out = c_w_out16
        y = jnp.concatenate([yp, ys], axis=0)
        h = matmul_postnorm_residual(y, w_out, j, g[1], h, tm=tm)
        hm = norm_swiglu(h, g[2], ffn_w_gate_up, layer, tm=tm_in, tn=_col_tile(dff, (512, 256, 128)))
        h = matmul_postnorm_residual(hm, ffn_w_down16, layer, g[3], h, tm=tm)
    return (h[:mp].reshape(nb, seq, d), h[mp:].reshape(nbs, tdec, d),
            jnp.stack(kv_p), jnp.stack(kv_s), jnp.stack(hg_p), jnp.stack(hg_s),
            jnp.stack(win_p), jnp.stack(win_s))
```

```python
import functools
import math

import numpy as np
import jax
import jax.numpy as jnp
from jax import lax
from jax.experimental import pallas as pl
from jax.experimental.pallas import tpu as pltpu

F32 = jnp.float32
BF16 = jnp.bfloat16

NORM_EPS = 1e-6
NUM_BUCKETS = 32
MAX_DISTANCE = 128
NEG = -1e30
LANES = 128
HALF = LANES // 2
MIB = 1024 * 1024

HEAD_DIM_A = 64
N_KV_A = 4
GROUP_A = 4
HEAD_DIM_C = 64
N_KV_C = 4
GROUP_C = 8
WINDOW = 128
TQ_A = 256
PAGES_PER_STEP = 32
PAGE_GROUP = 32
CHUNK_B = 128
SUB_B = 16
ROWS_B = 512
HEADS_B = 4


def _cparams(sem, vmem_mib):
    return pltpu.CompilerParams(dimension_semantics=sem, vmem_limit_bytes=vmem_mib * MIB)


def _rms(x, g):
    ms = jnp.mean(x * x, axis=-1, keepdims=True)
    return x * lax.rsqrt(ms + NORM_EPS) * g


def _sigmoid(x):
    return 1.0 / (1.0 + jnp.exp(-x))


def _dot(a, b):
    return jnp.dot(a, b, preferred_element_type=F32)


def _dot_nt(a, b):
    return lax.dot_general(a, b, (((1,), (1,)), ((), ())), preferred_element_type=F32)


def _norm_matmul_kernel(x_ref, g_ref, w_ref, o_ref, *rest, n_tail):
    xn_ref = rest[-1]
    j = pl.program_id(1)

    @pl.when(j == 0)
    def _():
        xn_ref[...] = _rms(x_ref[...], g_ref[...]).astype(BF16)

    res = _dot(xn_ref[...], w_ref[...])
    o_ref[...] = res
    if n_tail:
        tail_ref = rest[0]
        tm = o_ref.shape[0]

        @pl.when(j == pl.num_programs(1) - 1)
        def _():
            for slot in range(n_tail):
                tail_ref[pl.ds(slot, tm, stride=n_tail), :] = res[:, slot * LANES:(slot + 1) * LANES]


def norm_matmul(x, g, w, layer, *, tm, tn, tail_rows=False):
    m, k = x.shape
    n = w.shape[2]
    assert m % tm == 0 and n % tn == 0
    n_tail = tn // LANES if tail_rows else 0
    out_specs = [pl.BlockSpec((tm, tn), lambda i, j: (i, j))]
    out_shape = [jax.ShapeDtypeStruct((m, n), F32)]
    if tail_rows:
        out_specs.append(pl.BlockSpec((tm * n_tail, LANES), lambda i, j: (i, 0)))
        out_shape.append(jax.ShapeDtypeStruct((m * n_tail, LANES), F32))
    out = pl.pallas_call(
        functools.partial(_norm_matmul_kernel, n_tail=n_tail),
        grid=(m // tm, n // tn),
        in_specs=[pl.BlockSpec((tm, k), lambda i, j: (i, 0)),
                  pl.BlockSpec((1, k), lambda i, j: (0, 0)),
                  pl.BlockSpec((None, k, tn), lambda i, j: (layer, 0, j))],
        out_specs=out_specs,
        out_shape=out_shape,
        scratch_shapes=[pltpu.VMEM((tm, k), BF16)],
        compiler_params=_cparams(("parallel", "arbitrary"), 56),
        name="norm_matmul",
    )(x, g.reshape(1, k), w)
    return out if tail_rows else out[0]


def _norm_swiglu_kernel(x_ref, g_ref, wg_ref, wu_ref, o_ref, xn_ref):
    @pl.when(pl.program_id(1) == 0)
    def _():
        xn_ref[...] = _rms(x_ref[...], g_ref[...]).astype(BF16)

    xn = xn_ref[...]
    gate = _dot(xn, wg_ref[...].astype(BF16))
    up = _dot(xn, wu_ref[...].astype(BF16))
    o_ref[...] = (gate * _sigmoid(gate) * up).astype(o_ref.dtype)


def norm_swiglu(x, g, w_gate_up, layer, *, tm, tn):
    m, k = x.shape
    dff = w_gate_up.shape[2] // 2
    assert m % tm == 0 and dff % tn == 0
    nj = dff // tn
    return pl.pallas_call(
        _norm_swiglu_kernel,
        grid=(m // tm, nj),
        in_specs=[pl.BlockSpec((tm, k), lambda i, j: (i, 0)),
                  pl.BlockSpec((1, k), lambda i, j: (0, 0)),
                  pl.BlockSpec((None, k, tn), lambda i, j: (layer, 0, j)),
                  pl.BlockSpec((None, k, tn), lambda i, j: (layer, 0, j + nj))],
        out_specs=pl.BlockSpec((tm, tn), lambda i, j: (i, j)),
        out_shape=jax.ShapeDtypeStruct((m, dff), BF16),
        scratch_shapes=[pltpu.VMEM((tm, k), BF16)],
        compiler_params=_cparams(("parallel", "arbitrary"), 56),
        name="norm_swiglu",
    )(x, g.reshape(1, k), w_gate_up, w_gate_up)


def _matmul_postnorm_kernel(*refs, nk, off, has_tail, split_out):
    refs = list(refs)
    y_ref = refs.pop(0)
    yt_ref = refs.pop(0) if has_tail else None
    w_ref, g_ref, h_ref, o_ref = refs[:4]
    ot_ref = refs[4] if split_out else None
    acc_ref = refs[-1]
    k = pl.program_id(1)
    last_tile = pl.program_id(0) == pl.num_programs(0) - 1

    def body(last):
        def part():
            if last and has_tail:
                y = jnp.concatenate([y_ref[:off, :], yt_ref[...]], axis=0)
            else:
                y = y_ref[...]
            return _dot(y, w_ref[...])

        def finish(acc):
            res = h_ref[...] + _rms(acc, g_ref[...])
            o_ref[...] = res
            if last and split_out:
                ot_ref[...] = res[off:]

        if nk == 1:
            finish(part())
            return

        @pl.when(k == 0)
        def _():
            acc_ref[...] = part()

        @pl.when(jnp.logical_and(k > 0, k < nk - 1))
        def _():
            acc_ref[...] += part()

        @pl.when(k == nk - 1)
        def _():
            finish(acc_ref[...] + part())

    if not (has_tail or split_out):
        body(False)
        return
    pl.when(jnp.logical_not(last_tile))(lambda: body(False))
    pl.when(last_tile)(lambda: body(True))


def matmul_postnorm_residual(y, w, layer, g, h, *, tm, y_tail=None, split_lead=None):
    m, n = h.shape
    kdim = y.shape[1]
    tk = kdim if kdim <= 2048 else _col_tile(kdim, (1408, 1024, 512, 256, 128))
    assert m % tm == 0 and kdim % tk == 0
    nk = kdim // tk
    ni = m // tm
    has_tail = y_tail is not None
    split_out = split_lead is not None
    lead = y.shape[0] if has_tail else (split_lead if split_out else m)
    off = lead - (ni - 1) * tm
    if has_tail or split_out:
        assert 0 < off < tm
        assert not has_tail or (y_tail.shape[0] == tm - off and lead + y_tail.shape[0] == m)
        assert not split_out or split_lead == lead
    in_specs = [pl.BlockSpec((tm, tk), lambda i, k: (i, k))]
    args = [y]
    if has_tail:
        in_specs.append(pl.BlockSpec((tm - off, tk), lambda i, k: (0, k)))
        args.append(y_tail)
    in_specs += [pl.BlockSpec((None, tk, n), lambda i, k: (layer, k, 0)),
                 pl.BlockSpec((1, n), lambda i, k: (0, 0)),
                 pl.BlockSpec((tm, n), lambda i, k: (i, 0))]
    args += [w, g.reshape(1, n), h]
    if split_out:
        out_specs = [pl.BlockSpec((tm, n), lambda i, k: (i, 0)), pl.BlockSpec((tm - off, n), lambda i, k: (0, 0))]
        out_shape = [jax.ShapeDtypeStruct((lead, n), F32), jax.ShapeDtypeStruct((tm - off, n), F32)]
    else:
        out_specs = pl.BlockSpec((tm, n), lambda i, k: (i, 0))
        out_shape = jax.ShapeDtypeStruct((m, n), F32)
    return pl.pallas_call(
        functools.partial(_matmul_postnorm_kernel, nk=nk, off=off, has_tail=has_tail, split_out=split_out),
        grid=(ni, nk),
        in_specs=in_specs,
        out_specs=out_specs,
        out_shape=out_shape,
        scratch_shapes=[pltpu.VMEM((tm, n) if nk > 1 else (8, LANES), F32)],
        compiler_params=_cparams(("parallel", "arbitrary"), 56),
        name="matmul_postnorm_residual",
    )(*args)


def _bucket_np(dist):
    n = np.maximum(dist, 0)
    max_exact = NUM_BUCKETS // 2
    ratio = np.log(np.maximum(n, 1).astype(np.float32) / np.float32(max_exact)) / np.float32(
        math.log(MAX_DISTANCE / max_exact))
    large = np.minimum(max_exact + (ratio * (NUM_BUCKETS - max_exact)).astype(np.int32), NUM_BUCKETS - 1)
    return np.where(n < max_exact, n, large).astype(np.int32)


def _bias_tile_kernel(tab_ref, bucket_ref, mask_ref, o_ref):
    c = pl.program_id(0)
    b = bucket_ref[...]
    acc = jnp.zeros(b.shape, F32)
    for k in range(NUM_BUCKETS):
        acc = jnp.where(b == k, tab_ref[k, c], acc)
    o_ref[0] = jnp.where(mask_ref[...] > 0, acc, NEG)


def bias_tiles(table, dist, valid):
    ncol = table.shape[1]
    r, c = dist.shape
    return pl.pallas_call(
        _bias_tile_kernel,
        grid=(ncol,),
        in_specs=[pl.BlockSpec(memory_space=pltpu.SMEM),
                  pl.BlockSpec((r, c), lambda i: (0, 0)),
                  pl.BlockSpec((r, c), lambda i: (0, 0))],
        out_specs=pl.BlockSpec((1, r, c), lambda i: (i, 0, 0)),
        out_shape=jax.ShapeDtypeStruct((ncol, r, c), F32),
        compiler_params=_cparams(("parallel",), 32),
        name="bias_tiles",
    )(table, jnp.asarray(_bucket_np(dist)), jnp.asarray(valid.astype(np.int32)))


def _band_dist(t):
    return np.arange(t)[:, None] + t - np.arange(2 * t)[None, :]


def _decode_dist(tdec, n_past, cols):
    t = np.arange(8)[:, None]
    k = np.arange(cols)[None, :]
    dist = np.where(k < n_past, t + n_past - k, t - (k - n_past))
    inside = (k < n_past + tdec) & (t < tdec)
    return dist, inside


def _diff_lambda(lq_ref, lk_ref, lam_init):
    e = jnp.exp(jnp.sum(lq_ref[...] * lk_ref[...], axis=-1, keepdims=True))
    return e[0:1] - e[1:2] + lam_init


def _split_maps(q):
    lane = lax.broadcasted_iota(jnp.int32, q.shape, 1)
    return jnp.where(lane < HALF, q, 0.0), jnp.where(lane >= HALF, q, 0.0)


def _attn_a_prompt_kernel(tab_ref, lq_ref, lk_ref, q_ref, k_ref, v_ref, bias_ref, sg_ref, o_ref,
                          k_scr, v_scr, s_scr, m_scr, l_scr, acc_scr, *, lam_init, scale):
    kvh = pl.program_id(0)
    qi = pl.program_id(2)
    tq = q_ref.shape[0]
    nt = k_scr.shape[0]
    ng = GROUP_A
    nrg = 2 * ng

    @pl.when(qi == 0)
    def _():
        for t in range(nt):
            k_scr[t] = k_ref[t * tq:(t + 1) * tq, :].astype(BF16)
            v_scr[t] = v_ref[t * tq:(t + 1) * tq, :].astype(BF16)

    q = q_ref[...] * scale
    parts1, parts2 = [], []
    for g in range(ng):
        q1, q2 = _split_maps(q[:, g * LANES:(g + 1) * LANES])
        parts1.append(q1)
        parts2.append(q2)
    qs = jnp.concatenate(parts1 + parts2, axis=0).astype(BF16)

    def score_tile(t, bias_of_group, first=False):
        s = _dot_nt(qs, k_scr[t])
        for rg in range(nrg):
            rows = slice(rg * tq, (rg + 1) * tq)
            sg = s[rows] + bias_of_group(rg)
            s_scr[t, rows, :] = sg
            mx = sg[:, :LANES]
            for c in range(1, tq // LANES):
                mx = jnp.maximum(mx, sg[:, c * LANES:(c + 1) * LANES])
            m_scr[rows, :] = mx if first else jnp.maximum(m_scr[rows, :], mx)

    def col_of_group(rg):
        m, g = divmod(rg, ng)
        return 2 * g + m

    score_tile(qi, lambda rg: bias_ref[col_of_group(rg), :, tq:2 * tq], first=True)

    def far_body(t, carry):
        score_tile(t, lambda rg: tab_ref[NUM_BUCKETS - 1, kvh * nrg + col_of_group(rg)])
        return carry

    lax.fori_loop(0, qi - 1, far_body, 0)

    @pl.when(qi >= 1)
    def _():
        score_tile(qi - 1, lambda rg: bias_ref[col_of_group(rg), :, 0:tq])

    m_row = jnp.max(m_scr[...], axis=-1, keepdims=True)

    def pv_tile(t, first=False):
        p = jnp.exp(s_scr[t] - m_row)
        ps = p[:, :LANES]
        for c in range(1, tq // LANES):
            ps = ps + p[:, c * LANES:(c + 1) * LANES]
        pv = _dot(p.astype(BF16), v_scr[t])
        if first:
            l_scr[...] = ps
            acc_scr[...] = pv
        else:
            l_scr[...] += ps
            acc_scr[...] += pv

    pv_tile(qi, first=True)

    def pv_body(t, carry):
        pv_tile(t)
        return carry

    lax.fori_loop(0, qi, pv_body, 0)

    l_row = jnp.sum(l_scr[...], axis=-1, keepdims=True)
    o_all = acc_scr[...] / l_row
    half = ng * tq
    lam = _diff_lambda(lq_ref, lk_ref, lam_init)
    o = o_all[:half] - lam * o_all[half:]
    for g in range(ng):
        og = _rms(o[g * tq:(g + 1) * tq], sg_ref[...]) * (1.0 - lam_init)
        o_ref[:, g * LANES:(g + 1) * LANES] = og.astype(o_ref.dtype)


def attn_a_prompt(qkv, nb, seq, table, bias_near, lam_q, lam_k, subln_g, lam_init):
    tq = TQ_A
    nq = seq // tq
    nqb = (N_KV_A * GROUP_A * LANES) // LANES
    kern = functools.partial(_attn_a_prompt_kernel, lam_init=lam_init, scale=HEAD_DIM_A ** -0.5)
    nrow = 2 * GROUP_A * tq
    return pl.pallas_call(
        kern,
        grid=(N_KV_A, nb, nq),
        in_specs=[pl.BlockSpec(memory_space=pltpu.SMEM),
                  pl.BlockSpec((2, HEAD_DIM_A), lambda h, b, i: (0, 0)),
                  pl.BlockSpec((2, HEAD_DIM_A), lambda h, b, i: (0, 0)),
                  pl.BlockSpec((tq, GROUP_A * LANES), lambda h, b, i: (b * nq + i, h)),
                  pl.BlockSpec((seq, LANES), lambda h, b, i: (b, nqb + h)),
                  pl.BlockSpec((seq, LANES), lambda h, b, i: (b, nqb + N_KV_A + h)),
                  pl.BlockSpec((2 * GROUP_A, tq, 2 * tq), lambda h, b, i: (h, 0, 0)),
                  pl.BlockSpec((1, LANES), lambda h, b, i: (0, 0))],
        out_specs=pl.BlockSpec((tq, GROUP_A * LANES), lambda h, b, i: (b * nq + i, h)),
        out_shape=jax.ShapeDtypeStruct((nb * seq, N_KV_A * GROUP_A * LANES), BF16),
        scratch_shapes=[pltpu.VMEM((nq, tq, LANES), BF16),
                        pltpu.VMEM((nq, tq, LANES), BF16),
                        pltpu.VMEM((nq, nrow, tq), F32),
                        pltpu.VMEM((nrow, LANES), F32),
                        pltpu.VMEM((nrow, LANES), F32),
                        pltpu.VMEM((nrow, LANES), F32)],
        compiler_params=_cparams(("parallel", "parallel", "arbitrary"), 56),
        name="attn_a_prompt",
    )(table, lam_q, lam_k, qkv, qkv, qkv, bias_near, subln_g.reshape(1, LANES))


def _attn_a_sample_kernel(pt_ref, lq_ref, lk_ref, q_ref, *rest, lam_init, scale, npp):
    page_refs = rest[:npp]
    (knew_ref, bfar_ref, blast_ref, bnew_ref, sg_ref, o_ref, m_scr, l_scr, acc_scr) = rest[npp:]
    ps = pl.program_id(1)
    nsteps = pl.num_programs(1)
    nkv = N_KV_A
    tdec = knew_ref.shape[1]

    @pl.when(ps == 0)
    def _():
        m_scr[...] = jnp.full(m_scr.shape, -jnp.inf, F32)
        l_scr[...] = jnp.zeros(l_scr.shape, F32)
        acc_scr[...] = jnp.zeros(acc_scr.shape, F32)

    qs32 = []
    for h in range(nkv):
        q1, q2 = _split_maps(q_ref[0, h] * scale)
        qs32.append(jnp.concatenate([q1, q2], axis=0))
    qs = [x.astype(BF16) for x in qs32]
    is_last = ps == nsteps - 1

    nslot = 2 * nkv
    page = page_refs[0].shape[0] // nslot

    def slot_rows(slot, refs):
        rows = [r[pl.ds(slot, page, stride=nslot), :] for r in refs]
        return jnp.concatenate(rows, axis=0).astype(BF16)

    grp = min(PAGE_GROUP, npp)
    for g0 in range(0, npp, grp):
        refs = page_refs[g0:g0 + grp]
        for h in range(nkv):
            kh = slot_rows(h, refs)
            vh = slot_rows(nkv + h, refs)
            s = _dot_nt(qs[h], kh)
            bfar = bfar_ref[h]
            if g0 + grp == npp:
                tail = jnp.where(is_last, blast_ref[h], bfar)
                s = jnp.concatenate([s[:, :-page] + bfar[:, 0:1], s[:, -page:] + tail], axis=1)
            else:
                s = s + bfar[:, 0:1]
            m_prev = m_scr[h]
            m_new = jnp.maximum(m_prev, jnp.max(s, axis=-1, keepdims=True))
            alpha = jnp.exp(m_prev - m_new)
            p = jnp.exp(s - m_new[:, 0:1])
            l_scr[h] = alpha * l_scr[h] + jnp.sum(p, axis=-1, keepdims=True)
            acc_scr[h] = alpha * acc_scr[h] + _dot(p.astype(BF16), vh)
            m_scr[h] = m_new

    @pl.when(is_last)
    def _():
        lam = _diff_lambda(lq_ref, lk_ref, lam_init)
        knew = knew_ref[0]
        for h in range(nkv):
            kn = knew[:, h * LANES:(h + 1) * LANES]
            vn = knew[:, (nkv + h) * LANES:(nkv + h + 1) * LANES]
            bnew = bnew_ref[h]
            cols = [jnp.sum(qs32[h] * kn[t:t + 1, :], axis=-1, keepdims=True) + bnew[:, t:t + 1]
                    for t in range(tdec)]
            m_prev = m_scr[h]
            m_new = m_prev
            for c in cols:
                m_new = jnp.maximum(m_new, c)
            alpha = jnp.exp(m_prev - m_new)
            l = alpha * l_scr[h]
            acc = alpha * acc_scr[h]
            for t in range(tdec):
                p = jnp.exp(cols[t] - m_new)
                l = l + p
                acc = acc + p * vn[t:t + 1, :]
            o_all = acc / l
            half = o_all.shape[0] // 2
            o = o_all[:half] - lam * o_all[half:]
            o_ref[0, h] = (_rms(o, sg_ref[...]) * (1.0 - lam_init)).astype(o_ref.dtype)


def attn_a_sample(q_r, cache, layer, page_table, knew, bias_far, bias_last, bias_new,
                  lam_q, lam_k, subln_g, lam_init):
    nb, nkv, ngt, _ = q_r.shape
    n_pages = page_table.shape[1]
    page_rows = cache.shape[2]
    assert page_rows == 2 * nkv * LANES and cache.shape[3] == LANES
    npp = min(PAGES_PER_STEP, n_pages)
    assert n_pages % npp == 0
    tdec = knew.shape[1]
    width = knew.shape[2]
    kern = functools.partial(_attn_a_sample_kernel, lam_init=lam_init, scale=HEAD_DIM_A ** -0.5, npp=npp)

    def page_spec(i):
        return pl.BlockSpec((None, None, page_rows, LANES),
                            lambda b, p, pt: (layer, pt[b, p * npp + i], 0, 0))

    const3 = lambda b, p, pt: (0, 0, 0)
    grid_spec = pltpu.PrefetchScalarGridSpec(
        num_scalar_prefetch=1,
        grid=(nb, n_pages // npp),
        in_specs=[pl.BlockSpec((2, HEAD_DIM_A), lambda b, p, pt: (0, 0)),
                  pl.BlockSpec((2, HEAD_DIM_A), lambda b, p, pt: (0, 0)),
                  pl.BlockSpec((1, nkv, ngt, LANES), lambda b, p, pt: (b, 0, 0, 0))]
                 + [page_spec(i) for i in range(npp)]
                 + [pl.BlockSpec((1, tdec, width), lambda b, p, pt: (b, 0, 0)),
                    pl.BlockSpec((nkv, 2 * ngt, LANES), const3),
                    pl.BlockSpec((nkv, 2 * ngt, LANES), const3),
                    pl.BlockSpec((nkv, 2 * ngt, LANES), const3),
                    pl.BlockSpec((1, LANES), lambda b, p, pt: (0, 0))],
        out_specs=pl.BlockSpec((1, nkv, ngt, LANES), lambda b, p, pt: (b, 0, 0, 0)),
        scratch_shapes=[pltpu.VMEM((nkv, 2 * ngt, LANES), F32),
                        pltpu.VMEM((nkv, 2 * ngt, LANES), F32),
                        pltpu.VMEM((nkv, 2 * ngt, LANES), F32)],
    )
    return pl.pallas_call(
        kern,
        grid_spec=grid_spec,
        out_shape=jax.ShapeDtypeStruct((nb, nkv, ngt, LANES), BF16),
        compiler_params=_cparams(("parallel", "arbitrary"), 56),
        name="attn_a_sample",
    )(page_table, lam_q, lam_k, q_r, *([cache] * npp), knew, bias_far, bias_last, bias_new,
      subln_g.reshape(1, LANES))


def _dup_half(x, odd):
    lane = lax.broadcasted_iota(jnp.int32, x.shape, 1)
    rolled = pltpu.roll(x, HALF, axis=1)
    keep = (lane >= HALF) if odd else (lane < HALF)
    return jnp.where(keep, x, rolled)


def _swa_prompt_kernel(sink_ref, q_ref, kvp_ref, kvo_ref, bias_ref, o_ref, *, scale):
    n = pl.program_id(1)
    w = q_ref.shape[0]
    ng = GROUP_C
    lane = lax.broadcasted_iota(jnp.int32, (w, LANES), 1)
    kcol = lax.broadcasted_iota(jnp.int32, (2 * w, 2 * w), 1)
    first = n == 0
    vpair0 = (N_KV_C * HEAD_DIM_C) // LANES
    ones = jnp.ones((2 * w, LANES), BF16)

    def kv_pair(c):
        cols = slice(c * LANES, (c + 1) * LANES)
        return jnp.concatenate([kvp_ref[:, cols], kvo_ref[:, cols]], axis=0)

    for h in range(N_KV_C):
        kd = _dup_half(kv_pair(h // 2), h % 2).astype(BF16)
        vd = _dup_half(kv_pair(vpair0 + h // 2), h % 2).astype(BF16)
        vd1 = jnp.concatenate([vd, ones], axis=1)
        for j in range(ng // 2):
            hd = h * ng + 2 * j
            col = (hd // 2) * LANES
            qp = q_ref[:, col:col + LANES] * scale
            qs = jnp.concatenate([jnp.where(lane < HALF, qp, 0.0), jnp.where(lane >= HALF, qp, 0.0)], axis=0)
            sink = jnp.concatenate([jnp.full((w, LANES), sink_ref[hd], F32),
                                    jnp.full((w, LANES), sink_ref[hd + 1], F32)], axis=0)
            bias = bias_ref[hd:hd + 2].reshape(2 * w, 2 * w)
            bias = jnp.where(jnp.logical_and(first, kcol < w), NEG, bias)
            s = _dot_nt(qs.astype(BF16), kd) + bias
            m = jnp.maximum(jnp.max(s, axis=-1, keepdims=True), sink)
            p = jnp.concatenate([jnp.exp(s[:, c * LANES:(c + 1) * LANES] - m) for c in range(2 * w // LANES)],
                                axis=1)
            ov = _dot(p.astype(BF16), vd1)
            o = ov[:, :LANES] / (ov[:, LANES:] + jnp.exp(sink - m))
            o_ref[:, col:col + LANES] = jnp.where(lane < HALF, o[:w], o[w:]).astype(o_ref.dtype)


def swa_prompt(qkv, nb, seq, sinks, bias_swa):
    w = WINDOW
    nblk = seq // w
    dq = N_KV_C * GROUP_C * HEAD_DIM_C
    dkv = 2 * N_KV_C * HEAD_DIM_C
    kern = functools.partial(_swa_prompt_kernel, scale=HEAD_DIM_C ** -0.5)
    return pl.pallas_call(
        kern,
        grid=(nb, nblk),
        in_specs=[pl.BlockSpec(memory_space=pltpu.SMEM),
                  pl.BlockSpec((w, dq), lambda b, n: (b * nblk + n, 0)),
                  pl.BlockSpec((w, dkv), lambda b, n: (b * nblk + jnp.maximum(n - 1, 0), dq // dkv)),
                  pl.BlockSpec((w, dkv), lambda b, n: (b * nblk + n, dq // dkv)),
                  pl.BlockSpec((N_KV_C * GROUP_C, w, 2 * w), lambda b, n: (0, 0, 0))],
        out_specs=pl.BlockSpec((w, dq), lambda b, n: (b * nblk + n, 0)),
        out_shape=jax.ShapeDtypeStruct((nb * seq, dq), BF16),
        compiler_params=_cparams(("parallel", "parallel"), 48),
        name="swa_prompt",
    )(sinks, qkv, qkv, qkv, bias_swa)


def _swa_sample_kernel(q_ref, k_ref, v_ref, bias_ref, sink_ref, o_ref, *, scale):
    for h in range(N_KV_C):
        q = (q_ref[0, h] * scale).astype(BF16)
        s = _dot_nt(q, k_ref[0, h].astype(BF16)) + bias_ref[h]
        sink = sink_ref[h][:, 0:1]
        m = jnp.maximum(jnp.max(s, axis=-1, keepdims=True), sink)
        p = jnp.exp(s - m)
        den = jnp.sum(p, axis=-1, keepdims=True) + jnp.exp(sink - m)
        o_ref[0, h] = (_dot(p.astype(BF16), v_ref[0, h].astype(BF16)) / den).astype(o_ref.dtype)


def swa_sample(q_r, k_r, v_r, bias, sink_rows):
    nb, nkv, ngt, d = q_r.shape
    keys = k_r.shape[2]
    kern = functools.partial(_swa_sample_kernel, scale=HEAD_DIM_C ** -0.5)
    return pl.pallas_call(
        kern,
        grid=(nb,),
        in_specs=[pl.BlockSpec((1, nkv, ngt, d), lambda b: (b, 0, 0, 0)),
                  pl.BlockSpec((1, nkv, keys, d), lambda b: (b, 0, 0, 0)),
                  pl.BlockSpec((1, nkv, keys, d), lambda b: (b, 0, 0, 0)),
                  pl.BlockSpec((nkv, ngt, keys), lambda b: (0, 0, 0)),
                  pl.BlockSpec((nkv, ngt, LANES), lambda b: (0, 0, 0))],
        out_specs=pl.BlockSpec((1, nkv, ngt, d), lambda b: (b, 0, 0, 0)),
        out_shape=jax.ShapeDtypeStruct((nb, nkv, ngt, d), BF16),
        compiler_params=_cparams(("parallel",), 32),
        name="swa_sample",
    )(q_r, k_r, v_r, bias, sink_rows)


def _lower_bound(lbp_ref, layer):
    x = lbp_ref[...]
    e = jnp.exp(x - jnp.max(x, axis=0, keepdims=True))
    den = jnp.sum(e, axis=0, keepdims=True)
    num = e[1:2]
    for l in range(2, layer + 1):
        num = num + e[l:l + 1]
    if layer == 0:
        num = jnp.zeros_like(den)
    return num / den


def _cumsum_rows(x):
    rows = x.shape[0]
    row = lax.broadcasted_iota(jnp.int32, x.shape, 0)
    sh = 1
    while sh < rows:
        x = x + jnp.where(row >= sh, pltpu.roll(x, sh, axis=0), 0.0)
        sh *= 2
    return x


def _gla_gates(qr, fr, lb):
    q = qr * _sigmoid(qr)
    f = lb + (1.0 - lb) * _sigmoid(fr)
    return q, 1.0 - f, jnp.log(f)


def _gla_scores(q, k, bcum, sub):
    rows = q.shape[0]
    sl = 8
    srow = lax.broadcasted_iota(jnp.int32, (sl, LANES), 0)
    lane = lax.broadcasted_iota(jnp.int32, (sl, LANES), 1)
    out = []
    for i in range(rows // sub):
        lo = i * sub
        qi, ki, bi = q[lo:lo + sub], k[lo:lo + sub], bcum[lo:lo + sub]
        a = jnp.zeros((sub, LANES), F32)
        if i > 0:
            ref = bcum[lo - 1:lo]
            qd = qi * jnp.exp(bi - ref)
            kd = jnp.concatenate([k[:lo] * jnp.exp(ref - bcum[:lo]), jnp.zeros((rows - lo, LANES), F32)], axis=0)
            a = _dot_nt(qd.astype(BF16), kd.astype(BF16))
            if rows < LANES:
                a = jnp.concatenate([a, jnp.zeros((sub, LANES - rows), F32)], axis=1)
        slabs = [a[r:r + sl] for r in range(0, sub, sl)]
        for s in range(sub):
            for n in range(s // sl, sub // sl):
                r = n * sl
                d = bi[r:r + sl] - bi[s:s + 1]
                if n == s // sl:
                    d = jnp.where(srow + r >= s, d, -jnp.inf)
                col = jnp.sum(qi[r:r + sl] * jnp.exp(d) * ki[s:s + 1], axis=-1, keepdims=True)
                slabs[n] = jnp.where(lane == lo + s, col, slabs[n])
        out.extend(slabs)
    return out[0] if len(out) == 1 else jnp.concatenate(out, axis=0)


def _pad_rows(x, rows):
    if x.shape[0] == rows:
        return x
    return jnp.concatenate([x, jnp.zeros((rows - x.shape[0], x.shape[1]), x.dtype)], axis=0)


def _gla_chunk(st, q, k, v, g, sub, last):
    bcum = _cumsum_rows(g)
    b_last = bcum[last:last + 1]
    o = _dot_nt((q * jnp.exp(bcum)).astype(BF16), st.astype(BF16))
    a = _gla_scores(q, k, bcum, sub)
    v128 = _pad_rows(v, LANES).astype(BF16)
    o = o + _dot(a.astype(BF16), v128)
    kdec = _pad_rows(k * jnp.exp(b_last - bcum), LANES).astype(BF16)
    vt = _pad_rows(v, LANES).T.astype(BF16)
    st_new = st * jnp.exp(b_last) + _dot(vt, kdec)
    return o, st_new


def _gla_finish(o, gate, ng_ref):
    return (_rms(o, ng_ref[...]) * (gate * _sigmoid(gate))).astype(BF16)


def _hgrn_prompt_kernel(lbp_ref, hq_ref, hf_ref, hi_ref, hg_ref, ng_ref, y_ref, sfin_ref, st_scr, *, layer):
    r = pl.program_id(2)
    c = CHUNK_B

    @pl.when(r == 0)
    def _():
        st_scr[...] = jnp.zeros(st_scr.shape, F32)

    nhs = st_scr.shape[0]
    lbs = [_lower_bound(lbp_ref.at[:, hd * LANES:(hd + 1) * LANES], layer) for hd in range(nhs)]

    def body(ci, carry):
        r0 = pl.multiple_of(ci * c, c)
        rows = pl.ds(r0, c)
        for hd in range(nhs):
            cols = slice(hd * LANES, (hd + 1) * LANES)
            q, k, g = _gla_gates(hq_ref[rows, cols], hf_ref[rows, cols], lbs[hd])
            o, st_new = _gla_chunk(st_scr[hd], q, k, hi_ref[rows, cols], g, SUB_B, c - 1)
            st_scr[hd] = st_new
            y_ref[rows, cols] = _gla_finish(o, hg_ref[rows, cols], ng_ref)
        return carry

    lax.fori_loop(0, hq_ref.shape[0] // c, body, 0)

    @pl.when(r == pl.num_programs(2) - 1)
    def _():
        for hd in range(nhs):
            sfin_ref[0, hd] = st_scr[hd].T


def hgrn_prompt(h_in, nb, seq, lb_param, norm_g, layer):
    nh = h_in.shape[1] // (4 * LANES)
    nhs = HEADS_B
    assert nh % nhs == 0
    ng = nh // nhs
    rows = min(ROWS_B, seq)
    nr = seq // rows
    depth = lb_param.shape[0]
    kern = functools.partial(_hgrn_prompt_kernel, layer=layer)

    def col(j):
        return pl.BlockSpec((rows, nhs * LANES), lambda b, h, r: (b * nr + r, j * ng + h))

    return pl.pallas_call(
        kern,
        grid=(nb, ng, nr),
        in_specs=[pl.BlockSpec((depth, nhs * LANES), lambda b, h, r: (0, h)),
                  col(0), col(1), col(2), col(3),
                  pl.BlockSpec((1, LANES), lambda b, h, r: (0, 0))],
        out_specs=[pl.BlockSpec((rows, nhs * LANES), lambda b, h, r: (b * nr + r, h)),
                   pl.BlockSpec((1, nhs, LANES, LANES), lambda b, h, r: (b, h, 0, 0))],
        out_shape=[jax.ShapeDtypeStruct((nb * seq, nh * LANES), BF16),
                   jax.ShapeDtypeStruct((nb, nh, LANES, LANES), F32)],
        scratch_shapes=[pltpu.VMEM((nhs, LANES, LANES), F32)],
        compiler_params=_cparams(("parallel", "parallel", "arbitrary"), 32),
        name="hgrn_prompt",
    )(lb_param, h_in, h_in, h_in, h_in, norm_g.reshape(1, LANES))


def _hgrn_sample_kernel(lbp_ref, h_ref, ng_ref, s0_ref, y_ref, s1_ref, *, layer, tdec):
    nh = s0_ref.shape[1]
    rows = h_ref.shape[1]
    for hd in range(nh):
        def cols(j, hd=hd):
            return slice((j * nh + hd) * LANES, (j * nh + hd + 1) * LANES)
        lb = _lower_bound(lbp_ref.at[:, cols(0)], layer)
        q, k, g = _gla_gates(h_ref[0, :, cols(0)], h_ref[0, :, cols(1)], lb)
        o, st_new = _gla_chunk(s0_ref[0, hd].T, q, k, h_ref[0, :, cols(2)], g, rows, tdec - 1)
        y_ref[0, :, cols(0)] = _gla_finish(o, h_ref[0, :, cols(3)], ng_ref)
        s1_ref[0, hd] = st_new.T


def hgrn_sample(h_in, state, state_layer, lb_param, norm_g, layer, tdec):
    nb, rows, width = h_in.shape
    nh = width // (4 * LANES)
    depth = lb_param.shape[0]
    kern = functools.partial(_hgrn_sample_kernel, layer=layer, tdec=tdec)
    return pl.pallas_call(
        kern,
        grid=(nb,),
        in_specs=[pl.BlockSpec((depth, nh * LANES), lambda b: (0, 0)),
                  pl.BlockSpec((1, rows, width), lambda b: (b, 0, 0)),
                  pl.BlockSpec((1, LANES), lambda b: (0, 0)),
                  pl.BlockSpec((None, 1, nh, LANES, LANES), lambda b: (state_layer, b, 0, 0, 0))],
        out_specs=[pl.BlockSpec((1, rows, nh * LANES), lambda b: (b, 0, 0)),
                   pl.BlockSpec((1, nh, LANES, LANES), lambda b: (b, 0, 0, 0))],
        out_shape=[jax.ShapeDtypeStruct((nb, rows, nh * LANES), BF16),
                   jax.ShapeDtypeStruct((nb, nh, LANES, LANES), F32)],
        compiler_params=_cparams(("parallel",), 32),
        name="hgrn_sample",
    )(lb_param, h_in, norm_g.reshape(1, LANES), state)


def _row_tile(m, cands=(640, 512, 256, 128, 64, 32, 16, 8)):
    for t in cands:
        if m % t == 0:
            return t
    raise ValueError(f"unsupported row count {m}")


def _col_tile(n, cands):
    for t in cands:
        if n % t == 0:
            return t
    raise ValueError(f"unsupported column count {n}")


def kernel(x_prompt, x_sample, cache_kv_a, state_hgrn_b, cache_win_c, page_table, norm_g, a_w_in, a_w_out, a_lambda, a_subln_g, b_w_in, b_w_out, b_lower_bound, b_norm_g, c_w_in, c_w_out, c_sinks, rel_bias_table, ffn_w_gate_up, ffn_w_down):
    nb, seq, d = x_prompt.shape
    nbs, tdec, _ = x_sample.shape
    depth = norm_g.shape[0]
    mp, ms = nb * seq, nbs * tdec
    m = mp + ms
    tm = _row_tile(m)
    tm_in = _row_tile(m, (1040, 640, 512, 256, 128, 64, 32, 16, 8))
    straddle = ms < tm and mp % tm == tm - ms
    page = cache_kv_a.shape[2]
    assert page == LANES and page >= MAX_DISTANCE and tdec <= 8
    kvw = N_KV_A * 2 * HEAD_DIM_A
    cache = cache_kv_a.reshape(cache_kv_a.shape[0], cache_kv_a.shape[1], page * 2 * N_KV_A, 2 * HEAD_DIM_A)
    table = rel_bias_table.astype(F32)
    ga, gc = GROUP_A, GROUP_C
    wb = cache_win_c.shape[2]
    keys_c = 2 * WINDOW

    dist_a = _band_dist(TQ_A)
    bias_a = bias_tiles(table, dist_a, dist_a >= 0)
    dist_c = _band_dist(WINDOW)
    bias_c = bias_tiles(table, dist_c, (dist_c >= 0) & (dist_c <= WINDOW))

    dist_as, in_as = _decode_dist(tdec, page, 2 * LANES)
    bias_as = bias_tiles(table, dist_as, in_as & (dist_as >= 0))
    bias_as = bias_as.reshape(N_KV_A, ga, 2, 8, 2 * LANES).transpose(0, 2, 1, 3, 4)[:, :, :, :tdec]
    bias_as = bias_as.reshape(N_KV_A, 2 * ga * tdec, 2 * LANES)
    last_a, new_a = bias_as[:, :, :LANES], bias_as[:, :, LANES:]
    far_a = table[NUM_BUCKETS - 1].reshape(N_KV_A, ga, 2).transpose(0, 2, 1)
    far_a = jnp.broadcast_to(far_a[:, :, :, None, None], (N_KV_A, 2, ga, tdec, LANES))
    far_a = far_a.reshape(N_KV_A, 2 * ga * tdec, LANES)
    dist_cs, in_cs = _decode_dist(tdec, wb, keys_c)
    bias_cs = bias_tiles(table, dist_cs, in_cs & (dist_cs >= 0) & (dist_cs <= WINDOW))
    bias_cs = bias_cs.reshape(N_KV_C, gc, 8, keys_c)[:, :, :tdec].reshape(N_KV_C, gc * tdec, keys_c)

    a_w_in16, a_w_out16 = a_w_in.astype(BF16), a_w_out.astype(BF16)
    b_w_in16, b_w_out16 = b_w_in.astype(BF16), b_w_out.astype(BF16)
    c_w_in16, c_w_out16 = c_w_in.astype(BF16), c_w_out.astype(BF16)
    ffn_w_down16 = ffn_w_down.astype(BF16)
    dff = ffn_w_gate_up.shape[2] // 2

    h = jnp.concatenate([x_prompt.reshape(mp, d), x_sample.reshape(ms, d)], axis=0)
    kv_p, kv_s, hg_p, hg_s, win_p, win_s = [], [], [], [], [], []
    for layer in range(depth):
        kind, j = layer % 3, layer // 3
        g = norm_g[layer]
        if kind == 0:
            lam_init = 0.8 - 0.6 * math.exp(-0.3 * layer)
            nslot = 2 * kvw // LANES
            qkv, kv_rows = norm_matmul(h, g[0], a_w_in16, j, tm=tm_in, tn=2 * kvw, tail_rows=True)
            dq = N_KV_A * GROUP_A * 2 * HEAD_DIM_A
            lam_q, lam_k = a_lambda[j][0::2], a_lambda[j][1::2]
            yp = attn_a_prompt(qkv, nb, seq, table, bias_a, lam_q, lam_k, a_subln_g[j], lam_init)
            qs = qkv[mp:, :dq].reshape(nbs, tdec, N_KV_A, ga, LANES).transpose(0, 2, 3, 1, 4)
            qs = qs.reshape(nbs, N_KV_A, ga * tdec, LANES)
            kvs = qkv[mp:, dq:].reshape(nbs, tdec, 2 * kvw)
            ys = attn_a_sample(qs, cache, j, page_table, kvs, far_a, last_a, new_a,
                               lam_q, lam_k, a_subln_g[j], lam_init)
            ys = ys.reshape(nbs, N_KV_A, ga, tdec, LANES).transpose(0, 3, 1, 2, 4).reshape(ms, dq)
            kv_p.append(kv_rows[:mp * nslot].reshape(nb, seq, 2, N_KV_A, 2 * HEAD_DIM_A))
            kv_s.append(kv_rows[mp * nslot:].reshape(nbs, tdec, 2, N_KV_A, 2 * HEAD_DIM_A))
            w_out = a_w_out16
        elif kind == 1:
            hin = norm_matmul(h, g[0], b_w_in16, j, tm=tm_in, tn=_col_tile(b_w_in.shape[2], (1024, 512, 256, 128)))
            yp, sp = hgrn_prompt(hin, nb, seq, b_lower_bound, b_norm_g[j], layer)
            hs_in = jnp.pad(hin[mp:].reshape(nbs, tdec, hin.shape[1]), ((0, 0), (0, 8 - tdec), (0, 0)))
            ys, ss = hgrn_sample(hs_in, state_hgrn_b, j, b_lower_bound, b_norm_g[j], layer, tdec)
            ys = ys[:, :tdec].reshape(ms, d)
            hg_p.append(sp)
            hg_s.append(ss)
            w_out = b_w_out16
        else:
            qkv = norm_matmul(h, g[0], c_w_in16, j, tm=tm_in, tn=_col_tile(c_w_in.shape[2], (1280, 512, 256, 128)))
            dq = N_KV_C * GROUP_C * HEAD_DIM_C
            dkv = N_KV_C * HEAD_DIM_C
            yp = swa_prompt(qkv, nb, seq, c_sinks[j], bias_c)
            kv_new = qkv[mp:, dq:].reshape(nbs, tdec, 2 * dkv)
            kv_all = jnp.concatenate([cache_win_c[j].reshape(nbs, wb, 2 * dkv), kv_new], axis=1)
            kv_pad = jnp.pad(kv_all, ((0, 0), (0, keys_c - wb - tdec), (0, 0)))
            kv_r = kv_pad.reshape(nbs, keys_c, 2, N_KV_C, HEAD_DIM_C).transpose(2, 0, 3, 1, 4)
            qs = qkv[mp:, :dq].reshape(nbs, tdec, N_KV_C, gc, HEAD_DIM_C).transpose(0, 2, 3, 1, 4)
            qs = qs.reshape(nbs, N_KV_C, gc * tdec, HEAD_DIM_C)
            sink_rows = jnp.broadcast_to(c_sinks[j].reshape(N_KV_C, gc, 1, 1), (N_KV_C, gc, tdec, LANES))
            sink_rows = sink_rows.reshape(N_KV_C, gc * tdec, LANES)
            ys = swa_sample(qs, kv_r[0], kv_r[1], bias_cs, sink_rows)
            ys = ys.reshape(nbs, N_KV_C, gc, tdec, HEAD_DIM_C).transpose(0, 3, 1, 2, 4).reshape(ms, dq)
            win_p.append(qkv[:mp, dq:].reshape(nb, seq, 2, N_KV_C, HEAD_DIM_C)[:, seq - wb:])
            win_s.append(kv_all[:, tdec:].reshape(nbs, wb, 2, N_KV_C, HEAD_DIM_C))
            w_out = c_w_out16
        if straddle:
            h = matmul_postnorm_residual(yp, w_out, j, g[1], h, tm=tm, y_tail=ys)
        else:
            h = matmul_postnorm_residual(jnp.concatenate([yp, ys], axis=0), w_out, j, g[1], h, tm=tm)
        hm = norm_swiglu(h, g[2], ffn_w_gate_up, layer, tm=tm_in, tn=_col_tile(dff, (512, 256, 128)))
        if straddle and layer == depth - 1:
            hp, hs = matmul_postnorm_residual(hm, ffn_w_down16, layer, g[3], h, tm=tm, split_lead=mp)
        else:
            h = matmul_postnorm_residual(hm, ffn_w_down16, layer, g[3], h, tm=tm)
            hp, hs = h[:mp], h[mp:]
    return (hp.reshape(nb, seq, d), hs.reshape(nbs, tdec, d),
            jnp.stack(kv_p), jnp.stack(kv_s), jnp.stack(hg_p), jnp.stack(hg_s),
            jnp.stack(win_p), jnp.stack(win_s))
```

```python
import functools
import math

import numpy as np
import jax
import jax.numpy as jnp
from jax import lax
from jax.experimental import pallas as pl
from jax.experimental.pallas import tpu as pltpu

F32 = jnp.float32
BF16 = jnp.bfloat16

NORM_EPS = 1e-6
NUM_BUCKETS = 32
MAX_DISTANCE = 128
NEG = -1e30
LANES = 128
HALF = LANES // 2
LOG2E = math.log2(math.e)
MIB = 1024 * 1024

HEAD_DIM_A = 64
N_KV_A = 4
GROUP_A = 4
HEAD_DIM_C = 64
N_KV_C = 4
GROUP_C = 8
WINDOW = 128
TQ_A = 256
PAGES_PER_STEP = 32
PAGE_GROUP = 32
CHUNK_B = 128
SUB_B = 16
ROWS_B = 512
HEADS_B = 4


def _cparams(sem, vmem_mib):
    return pltpu.CompilerParams(dimension_semantics=sem, vmem_limit_bytes=vmem_mib * MIB)


def _rms(x, g):
    ms = jnp.mean(x * x, axis=-1, keepdims=True)
    return x * lax.rsqrt(ms + NORM_EPS) * g


def _sigmoid(x):
    return 1.0 / (1.0 + jnp.exp(-x))


def _dot(a, b):
    return jnp.dot(a, b, preferred_element_type=F32)


def _dot_nt(a, b):
    return lax.dot_general(a, b, (((1,), (1,)), ((), ())), preferred_element_type=F32)


def _norm_matmul_kernel(x_ref, g_ref, w_ref, o_ref, *rest, n_tail):
    xn_ref = rest[-1]
    j = pl.program_id(1)

    @pl.when(j == 0)
    def _():
        xn_ref[...] = _rms(x_ref[...], g_ref[...]).astype(BF16)

    res = _dot(xn_ref[...], w_ref[...])
    o_ref[...] = res
    if n_tail:
        tail_ref = rest[0]
        tm = o_ref.shape[0]

        @pl.when(j == pl.num_programs(1) - 1)
        def _():
            for slot in range(n_tail):
                tail_ref[pl.ds(slot, tm, stride=n_tail), :] = res[:, slot * LANES:(slot + 1) * LANES]


def norm_matmul(x, g, w, layer, *, tm, tn, tail_rows=False):
    m, k = x.shape
    n = w.shape[2]
    assert m % tm == 0 and n % tn == 0
    n_tail = tn // LANES if tail_rows else 0
    out_specs = [pl.BlockSpec((tm, tn), lambda i, j: (i, j))]
    out_shape = [jax.ShapeDtypeStruct((m, n), F32)]
    if tail_rows:
        out_specs.append(pl.BlockSpec((tm * n_tail, LANES), lambda i, j: (i, 0)))
        out_shape.append(jax.ShapeDtypeStruct((m * n_tail, LANES), F32))
    out = pl.pallas_call(
        functools.partial(_norm_matmul_kernel, n_tail=n_tail),
        grid=(m // tm, n // tn),
        in_specs=[pl.BlockSpec((tm, k), lambda i, j: (i, 0)),
                  pl.BlockSpec((1, k), lambda i, j: (0, 0)),
                  pl.BlockSpec((None, k, tn), lambda i, j: (layer, 0, j))],
        out_specs=out_specs,
        out_shape=out_shape,
        scratch_shapes=[pltpu.VMEM((tm, k), BF16)],
        compiler_params=_cparams(("parallel", "arbitrary"), 56),
        name="norm_matmul",
    )(x, g.reshape(1, k), w)
    return out if tail_rows else out[0]


def _norm_swiglu_kernel(x_ref, g_ref, wg_ref, wu_ref, o_ref, xn_ref):
    @pl.when(pl.program_id(1) == 0)
    def _():
        xn_ref[...] = _rms(x_ref[...], g_ref[...]).astype(BF16)

    xn = xn_ref[...]
    gate = _dot(xn, wg_ref[...].astype(BF16))
    up = _dot(xn, wu_ref[...].astype(BF16))
    o_ref[...] = (gate * _sigmoid(gate) * up).astype(o_ref.dtype)


def norm_swiglu(x, g, w_gate_up, layer, *, tm, tn):
    m, k = x.shape
    dff = w_gate_up.shape[2] // 2
    assert m % tm == 0 and dff % tn == 0
    nj = dff // tn
    return pl.pallas_call(
        _norm_swiglu_kernel,
        grid=(m // tm, nj),
        in_specs=[pl.BlockSpec((tm, k), lambda i, j: (i, 0)),
                  pl.BlockSpec((1, k), lambda i, j: (0, 0)),
                  pl.BlockSpec((None, k, tn), lambda i, j: (layer, 0, j)),
                  pl.BlockSpec((None, k, tn), lambda i, j: (layer, 0, j + nj))],
        out_specs=pl.BlockSpec((tm, tn), lambda i, j: (i, j)),
        out_shape=jax.ShapeDtypeStruct((m, dff), BF16),
        scratch_shapes=[pltpu.VMEM((tm, k), BF16)],
        compiler_params=_cparams(("parallel", "arbitrary"), 56),
        name="norm_swiglu",
    )(x, g.reshape(1, k), w_gate_up, w_gate_up)


def _matmul_postnorm_kernel(*refs, nk, off, has_tail, split_out):
    refs = list(refs)
    y_ref = refs.pop(0)
    yt_ref = refs.pop(0) if has_tail else None
    w_ref, g_ref, h_ref, o_ref = refs[:4]
    ot_ref = refs[4] if split_out else None
    acc_ref = refs[-1]
    k = pl.program_id(1)
    last_tile = pl.program_id(0) == pl.num_programs(0) - 1

    def body(last):
        def part():
            if last and has_tail:
                y = jnp.concatenate([y_ref[:off, :], yt_ref[...]], axis=0)
            else:
                y = y_ref[...]
            return _dot(y, w_ref[...])

        def finish(acc):
            res = h_ref[...] + _rms(acc, g_ref[...])
            o_ref[...] = res
            if last and split_out:
                ot_ref[...] = res[off:]

        if nk == 1:
            finish(part())
            return

        @pl.when(k == 0)
        def _():
            acc_ref[...] = part()

        @pl.when(jnp.logical_and(k > 0, k < nk - 1))
        def _():
            acc_ref[...] += part()

        @pl.when(k == nk - 1)
        def _():
            finish(acc_ref[...] + part())

    if not (has_tail or split_out):
        body(False)
        return
    pl.when(jnp.logical_not(last_tile))(lambda: body(False))
    pl.when(last_tile)(lambda: body(True))


def matmul_postnorm_residual(y, w, layer, g, h, *, tm, y_tail=None, split_lead=None):
    m, n = h.shape
    kdim = y.shape[1]
    tk = kdim if kdim <= 2048 else _col_tile(kdim, (1408, 1024, 512, 256, 128))
    assert m % tm == 0 and kdim % tk == 0
    nk = kdim // tk
    ni = m // tm
    has_tail = y_tail is not None
    split_out = split_lead is not None
    lead = y.shape[0] if has_tail else (split_lead if split_out else m)
    off = lead - (ni - 1) * tm
    if has_tail or split_out:
        assert 0 < off < tm
        assert not has_tail or (y_tail.shape[0] == tm - off and lead + y_tail.shape[0] == m)
        assert not split_out or split_lead == lead
    in_specs = [pl.BlockSpec((tm, tk), lambda i, k: (i, k))]
    args = [y]
    if has_tail:
        in_specs.append(pl.BlockSpec((tm - off, tk), lambda i, k: (0, k)))
        args.append(y_tail)
    in_specs += [pl.BlockSpec((None, tk, n), lambda i, k: (layer, k, 0)),
                 pl.BlockSpec((1, n), lambda i, k: (0, 0)),
                 pl.BlockSpec((tm, n), lambda i, k: (i, 0))]
    args += [w, g.reshape(1, n), h]
    if split_out:
        out_specs = [pl.BlockSpec((tm, n), lambda i, k: (i, 0)), pl.BlockSpec((tm - off, n), lambda i, k: (0, 0))]
        out_shape = [jax.ShapeDtypeStruct((lead, n), F32), jax.ShapeDtypeStruct((tm - off, n), F32)]
    else:
        out_specs = pl.BlockSpec((tm, n), lambda i, k: (i, 0))
        out_shape = jax.ShapeDtypeStruct((m, n), F32)
    return pl.pallas_call(
        functools.partial(_matmul_postnorm_kernel, nk=nk, off=off, has_tail=has_tail, split_out=split_out),
        grid=(ni, nk),
        in_specs=in_specs,
        out_specs=out_specs,
        out_shape=out_shape,
        scratch_shapes=[pltpu.VMEM((tm, n) if nk > 1 else (8, LANES), F32)],
        compiler_params=_cparams(("parallel", "arbitrary"), 56),
        name="matmul_postnorm_residual",
    )(*args)


def _bucket_np(dist):
    n = np.maximum(dist, 0)
    max_exact = NUM_BUCKETS // 2
    ratio = np.log(np.maximum(n, 1).astype(np.float32) / np.float32(max_exact)) / np.float32(
        math.log(MAX_DISTANCE / max_exact))
    large = np.minimum(max_exact + (ratio * (NUM_BUCKETS - max_exact)).astype(np.int32), NUM_BUCKETS - 1)
    return np.where(n < max_exact, n, large).astype(np.int32)


def _bias_tile_kernel(tab_ref, bucket_ref, mask_ref, o_ref, *, unit):
    c = pl.program_id(0)
    b = bucket_ref[...]
    acc = jnp.zeros(b.shape, F32)
    for k in range(NUM_BUCKETS):
        acc = jnp.where(b == k, tab_ref[k, c] * unit, acc)
    o_ref[0] = jnp.where(mask_ref[...] > 0, acc, NEG)


def bias_tiles(table, dist, valid, unit=1.0):
    ncol = table.shape[1]
    r, c = dist.shape
    return pl.pallas_call(
        functools.partial(_bias_tile_kernel, unit=unit),
        grid=(ncol,),
        in_specs=[pl.BlockSpec(memory_space=pltpu.SMEM),
                  pl.BlockSpec((r, c), lambda i: (0, 0)),
                  pl.BlockSpec((r, c), lambda i: (0, 0))],
        out_specs=pl.BlockSpec((1, r, c), lambda i: (i, 0, 0)),
        out_shape=jax.ShapeDtypeStruct((ncol, r, c), F32),
        compiler_params=_cparams(("parallel",), 32),
        name="bias_tiles",
    )(table, jnp.asarray(_bucket_np(dist)), jnp.asarray(valid.astype(np.int32)))


def _band_dist(t):
    return np.arange(t)[:, None] + t - np.arange(2 * t)[None, :]


def _decode_dist(tdec, n_past, cols):
    t = np.arange(8)[:, None]
    k = np.arange(cols)[None, :]
    dist = np.where(k < n_past, t + n_past - k, t - (k - n_past))
    inside = (k < n_past + tdec) & (t < tdec)
    return dist, inside


def _diff_lambda(lq_ref, lk_ref, lam_init):
    e = jnp.exp(jnp.sum(lq_ref[...] * lk_ref[...], axis=-1, keepdims=True))
    return e[0:1] - e[1:2] + lam_init


def _split_maps(q):
    lane = lax.broadcasted_iota(jnp.int32, q.shape, 1)
    return jnp.where(lane < HALF, q, 0.0), jnp.where(lane >= HALF, q, 0.0)


def _attn_a_prompt_kernel(tab_ref, lq_ref, lk_ref, q_ref, k_ref, v_ref, bias_ref, sg_ref, o_ref,
                          k_scr, v_scr, s_scr, m_scr, acc_scr, *, lam_init, scale):
    kvh = pl.program_id(0)
    qi = pl.program_id(2)
    tq = q_ref.shape[0]
    nt = k_scr.shape[0]
    ng = GROUP_A
    nrg = 2 * ng

    @pl.when(qi == 0)
    def _():
        ones = jnp.ones((tq, LANES), BF16)
        for t in range(nt):
            k_scr[t] = k_ref[t * tq:(t + 1) * tq, :].astype(BF16)
            v_scr[t] = jnp.concatenate([v_ref[t * tq:(t + 1) * tq, :].astype(BF16), ones], axis=1)

    q = q_ref[...] * (scale * LOG2E)
    parts1, parts2 = [], []
    for g in range(ng):
        q1, q2 = _split_maps(q[:, g * LANES:(g + 1) * LANES])
        parts1.append(q1)
        parts2.append(q2)
    qs = jnp.concatenate(parts1 + parts2, axis=0).astype(BF16)

    def score_tile(t, bias_of_group, first=False):
        s = _dot_nt(qs, k_scr[t])
        for rg in range(nrg):
            rows = slice(rg * tq, (rg + 1) * tq)
            sg = s[rows] + bias_of_group(rg)
            s_scr[t, rows, :] = sg
            mx = sg[:, :LANES]
            for c in range(1, tq // LANES):
                mx = jnp.maximum(mx, sg[:, c * LANES:(c + 1) * LANES])
            m_scr[rows, :] = mx if first else jnp.maximum(m_scr[rows, :], mx)

    def col_of_group(rg):
        m, g = divmod(rg, ng)
        return 2 * g + m

    score_tile(qi, lambda rg: bias_ref[col_of_group(rg), :, tq:2 * tq], first=True)

    def far_body(t, carry):
        score_tile(t, lambda rg: tab_ref[NUM_BUCKETS - 1, kvh * nrg + col_of_group(rg)] * LOG2E)
        return carry

    lax.fori_loop(0, qi - 1, far_body, 0)

    @pl.when(qi >= 1)
    def _():
        score_tile(qi - 1, lambda rg: bias_ref[col_of_group(rg), :, 0:tq])

    m_row = jnp.max(m_scr[...], axis=-1, keepdims=True)

    def pv_tile(t, first=False):
        p = jnp.exp2(s_scr[t] - m_row)
        pv = _dot(p.astype(BF16), v_scr[t])
        if first:
            acc_scr[...] = pv
        else:
            acc_scr[...] += pv

    pv_tile(qi, first=True)

    def pv_body(t, carry):
        pv_tile(t)
        return carry

    lax.fori_loop(0, qi, pv_body, 0)

    acc = acc_scr[...]
    o_all = acc[:, :LANES] / acc[:, LANES:]
    half = ng * tq
    lam = _diff_lambda(lq_ref, lk_ref, lam_init)
    o = o_all[:half] - lam * o_all[half:]
    for g in range(ng):
        og = _rms(o[g * tq:(g + 1) * tq], sg_ref[...]) * (1.0 - lam_init)
        o_ref[:, g * LANES:(g + 1) * LANES] = og.astype(o_ref.dtype)


def attn_a_prompt(qkv, nb, seq, table, bias_near, lam_q, lam_k, subln_g, lam_init):
    tq = TQ_A
    nq = seq // tq
    nqb = (N_KV_A * GROUP_A * LANES) // LANES
    kern = functools.partial(_attn_a_prompt_kernel, lam_init=lam_init, scale=HEAD_DIM_A ** -0.5)
    nrow = 2 * GROUP_A * tq
    return pl.pallas_call(
        kern,
        grid=(N_KV_A, nb, nq),
        in_specs=[pl.BlockSpec(memory_space=pltpu.SMEM),
                  pl.BlockSpec((2, HEAD_DIM_A), lambda h, b, i: (0, 0)),
                  pl.BlockSpec((2, HEAD_DIM_A), lambda h, b, i: (0, 0)),
                  pl.BlockSpec((tq, GROUP_A * LANES), lambda h, b, i: (b * nq + i, h)),
                  pl.BlockSpec((seq, LANES), lambda h, b, i: (b, nqb + h)),
                  pl.BlockSpec((seq, LANES), lambda h, b, i: (b, nqb + N_KV_A + h)),
                  pl.BlockSpec((2 * GROUP_A, tq, 2 * tq), lambda h, b, i: (h, 0, 0)),
                  pl.BlockSpec((1, LANES), lambda h, b, i: (0, 0))],
        out_specs=pl.BlockSpec((tq, GROUP_A * LANES), lambda h, b, i: (b * nq + i, h)),
        out_shape=jax.ShapeDtypeStruct((nb * seq, N_KV_A * GROUP_A * LANES), BF16),
        scratch_shapes=[pltpu.VMEM((nq, tq, LANES), BF16),
                        pltpu.VMEM((nq, tq, 2 * LANES), BF16),
                        pltpu.VMEM((nq, nrow, tq), F32),
                        pltpu.VMEM((nrow, LANES), F32),
                        pltpu.VMEM((nrow, 2 * LANES), F32)],
        compiler_params=_cparams(("parallel", "parallel", "arbitrary"), 56),
        name="attn_a_prompt",
    )(table, lam_q, lam_k, qkv, qkv, qkv, bias_near, subln_g.reshape(1, LANES))


def _attn_a_sample_kernel(pt_ref, lq_ref, lk_ref, q_ref, *rest, lam_init, scale, npp):
    page_refs = rest[:npp]
    (knew_ref, bfar_ref, blast_ref, bnew_ref, sg_ref, o_ref, m_scr, l_scr, acc_scr) = rest[npp:]
    ps = pl.program_id(1)
    nsteps = pl.num_programs(1)
    nkv = N_KV_A
    tdec = knew_ref.shape[1]

    @pl.when(ps == 0)
    def _():
        m_scr[...] = jnp.full(m_scr.shape, -jnp.inf, F32)
        l_scr[...] = jnp.zeros(l_scr.shape, F32)
        acc_scr[...] = jnp.zeros(acc_scr.shape, F32)

    qs32 = []
    for h in range(nkv):
        q1, q2 = _split_maps(q_ref[0, h] * scale)
        qs32.append(jnp.concatenate([q1, q2], axis=0))
    qs = [x.astype(BF16) for x in qs32]
    is_last = ps == nsteps - 1

    nslot = 2 * nkv
    page = page_refs[0].shape[0] // nslot

    def slot_rows(slot, refs):
        rows = [r[pl.ds(slot, page, stride=nslot), :] for r in refs]
        return jnp.concatenate(rows, axis=0).astype(BF16)

    grp = min(PAGE_GROUP, npp)
    for g0 in range(0, npp, grp):
        refs = page_refs[g0:g0 + grp]
        for h in range(nkv):
            kh = slot_rows(h, refs)
            vh = slot_rows(nkv + h, refs)
            s = _dot_nt(qs[h], kh)
            bfar = bfar_ref[h]
            if g0 + grp == npp:
                tail = jnp.where(is_last, blast_ref[h], bfar)
                s = jnp.concatenate([s[:, :-page] + bfar[:, 0:1], s[:, -page:] + tail], axis=1)
            else:
                s = s + bfar[:, 0:1]
            m_prev = m_scr[h]
            m_new = jnp.maximum(m_prev, jnp.max(s, axis=-1, keepdims=True))
            alpha = jnp.exp(m_prev - m_new)
            p = jnp.exp(s - m_new[:, 0:1])
            l_scr[h] = alpha * l_scr[h] + jnp.sum(p, axis=-1, keepdims=True)
            acc_scr[h] = alpha * acc_scr[h] + _dot(p.astype(BF16), vh)
            m_scr[h] = m_new

    @pl.when(is_last)
    def _():
        lam = _diff_lambda(lq_ref, lk_ref, lam_init)
        knew = knew_ref[0]
        for h in range(nkv):
            kn = knew[:, h * LANES:(h + 1) * LANES]
            vn = knew[:, (nkv + h) * LANES:(nkv + h + 1) * LANES]
            bnew = bnew_ref[h]
            cols = [jnp.sum(qs32[h] * kn[t:t + 1, :], axis=-1, keepdims=True) + bnew[:, t:t + 1]
                    for t in range(tdec)]
            m_prev = m_scr[h]
            m_new = m_prev
            for c in cols:
                m_new = jnp.maximum(m_new, c)
            alpha = jnp.exp(m_prev - m_new)
            l = alpha * l_scr[h]
            acc = alpha * acc_scr[h]
            for t in range(tdec):
                p = jnp.exp(cols[t] - m_new)
                l = l + p
                acc = acc + p * vn[t:t + 1, :]
            o_all = acc / l
            half = o_all.shape[0] // 2
            o = o_all[:half] - lam * o_all[half:]
            o_ref[0, h] = (_rms(o, sg_ref[...]) * (1.0 - lam_init)).astype(o_ref.dtype)


def attn_a_sample(q_r, cache, layer, page_table, knew, bias_far, bias_last, bias_new,
                  lam_q, lam_k, subln_g, lam_init):
    nb, nkv, ngt, _ = q_r.shape
    n_pages = page_table.shape[1]
    page_rows = cache.shape[2]
    assert page_rows == 2 * nkv * LANES and cache.shape[3] == LANES
    npp = min(PAGES_PER_STEP, n_pages)
    assert n_pages % npp == 0
    tdec = knew.shape[1]
    width = knew.shape[2]
    kern = functools.partial(_attn_a_sample_kernel, lam_init=lam_init, scale=HEAD_DIM_A ** -0.5, npp=npp)

    def page_spec(i):
        return pl.BlockSpec((None, None, page_rows, LANES),
                            lambda b, p, pt: (layer, pt[b, p * npp + i], 0, 0))

    const3 = lambda b, p, pt: (0, 0, 0)
    grid_spec = pltpu.PrefetchScalarGridSpec(
        num_scalar_prefetch=1,
        grid=(nb, n_pages // npp),
        in_specs=[pl.BlockSpec((2, HEAD_DIM_A), lambda b, p, pt: (0, 0)),
                  pl.BlockSpec((2, HEAD_DIM_A), lambda b, p, pt: (0, 0)),
                  pl.BlockSpec((1, nkv, ngt, LANES), lambda b, p, pt: (b, 0, 0, 0))]
                 + [page_spec(i) for i in range(npp)]
                 + [pl.BlockSpec((1, tdec, width), lambda b, p, pt: (b, 0, 0)),
                    pl.BlockSpec((nkv, 2 * ngt, LANES), const3),
                    pl.BlockSpec((nkv, 2 * ngt, LANES), const3),
                    pl.BlockSpec((nkv, 2 * ngt, LANES), const3),
                    pl.BlockSpec((1, LANES), lambda b, p, pt: (0, 0))],
        out_specs=pl.BlockSpec((1, nkv, ngt, LANES), lambda b, p, pt: (b, 0, 0, 0)),
        scratch_shapes=[pltpu.VMEM((nkv, 2 * ngt, LANES), F32),
                        pltpu.VMEM((nkv, 2 * ngt, LANES), F32),
                        pltpu.VMEM((nkv, 2 * ngt, LANES), F32)],
    )
    return pl.pallas_call(
        kern,
        grid_spec=grid_spec,
        out_shape=jax.ShapeDtypeStruct((nb, nkv, ngt, LANES), BF16),
        compiler_params=_cparams(("parallel", "arbitrary"), 56),
        name="attn_a_sample",
    )(page_table, lam_q, lam_k, q_r, *([cache] * npp), knew, bias_far, bias_last, bias_new,
      subln_g.reshape(1, LANES))


def _dup_half(x, odd):
    lane = lax.broadcasted_iota(jnp.int32, x.shape, 1)
    rolled = pltpu.roll(x, HALF, axis=1)
    keep = (lane >= HALF) if odd else (lane < HALF)
    return jnp.where(keep, x, rolled)


def _swa_prompt_kernel(sink_ref, q_ref, kvp_ref, kvo_ref, bias_ref, o_ref, *, scale):
    n = pl.program_id(1)
    w = q_ref.shape[0]
    ng = GROUP_C
    lane = lax.broadcasted_iota(jnp.int32, (w, LANES), 1)
    kcol = lax.broadcasted_iota(jnp.int32, (2 * w, 2 * w), 1)
    first = n == 0
    vpair0 = (N_KV_C * HEAD_DIM_C) // LANES
    ones = jnp.ones((2 * w, LANES), BF16)

    def kv_pair(c):
        cols = slice(c * LANES, (c + 1) * LANES)
        return jnp.concatenate([kvp_ref[:, cols], kvo_ref[:, cols]], axis=0)

    for h in range(N_KV_C):
        kd = _dup_half(kv_pair(h // 2), h % 2).astype(BF16)
        vd = _dup_half(kv_pair(vpair0 + h // 2), h % 2).astype(BF16)
        vd1 = jnp.concatenate([vd, ones], axis=1)
        for j in range(ng // 2):
            hd = h * ng + 2 * j
            col = (hd // 2) * LANES
            qp = q_ref[:, col:col + LANES] * scale
            qs = jnp.concatenate([jnp.where(lane < HALF, qp, 0.0), jnp.where(lane >= HALF, qp, 0.0)], axis=0)
            sink = jnp.concatenate([jnp.full((w, LANES), sink_ref[hd], F32),
                                    jnp.full((w, LANES), sink_ref[hd + 1], F32)], axis=0)
            bias = bias_ref[hd:hd + 2].reshape(2 * w, 2 * w)
            bias = jnp.where(jnp.logical_and(first, kcol < w), NEG, bias)
            s = _dot_nt(qs.astype(BF16), kd) + bias
            m = jnp.maximum(jnp.max(s, axis=-1, keepdims=True), sink)
            p = jnp.concatenate([jnp.exp(s[:, c * LANES:(c + 1) * LANES] - m) for c in range(2 * w // LANES)],
                                axis=1)
            ov = _dot(p.astype(BF16), vd1)
            o = ov[:, :LANES] / (ov[:, LANES:] + jnp.exp(sink - m))
            o_ref[:, col:col + LANES] = jnp.where(lane < HALF, o[:w], o[w:]).astype(o_ref.dtype)


def swa_prompt(qkv, nb, seq, sinks, bias_swa):
    w = WINDOW
    nblk = seq // w
    dq = N_KV_C * GROUP_C * HEAD_DIM_C
    dkv = 2 * N_KV_C * HEAD_DIM_C
    kern = functools.partial(_swa_prompt_kernel, scale=HEAD_DIM_C ** -0.5)
    return pl.pallas_call(
        kern,
        grid=(nb, nblk),
        in_specs=[pl.BlockSpec(memory_space=pltpu.SMEM),
                  pl.BlockSpec((w, dq), lambda b, n: (b * nblk + n, 0)),
                  pl.BlockSpec((w, dkv), lambda b, n: (b * nblk + jnp.maximum(n - 1, 0), dq // dkv)),
                  pl.BlockSpec((w, dkv), lambda b, n: (b * nblk + n, dq // dkv)),
                  pl.BlockSpec((N_KV_C * GROUP_C, w, 2 * w), lambda b, n: (0, 0, 0))],
        out_specs=pl.BlockSpec((w, dq), lambda b, n: (b * nblk + n, 0)),
        out_shape=jax.ShapeDtypeStruct((nb * seq, dq), BF16),
        compiler_params=_cparams(("parallel", "parallel"), 48),
        name="swa_prompt",
    )(sinks, qkv, qkv, qkv, bias_swa)


def _swa_sample_kernel(q_ref, k_ref, v_ref, bias_ref, sink_ref, o_ref, *, scale):
    for h in range(N_KV_C):
        q = (q_ref[0, h] * scale).astype(BF16)
        s = _dot_nt(q, k_ref[0, h].astype(BF16)) + bias_ref[h]
        sink = sink_ref[h][:, 0:1]
        m = jnp.maximum(jnp.max(s, axis=-1, keepdims=True), sink)
        p = jnp.exp(s - m)
        den = jnp.sum(p, axis=-1, keepdims=True) + jnp.exp(sink - m)
        o_ref[0, h] = (_dot(p.astype(BF16), v_ref[0, h].astype(BF16)) / den).astype(o_ref.dtype)


def swa_sample(q_r, k_r, v_r, bias, sink_rows):
    nb, nkv, ngt, d = q_r.shape
    keys = k_r.shape[2]
    kern = functools.partial(_swa_sample_kernel, scale=HEAD_DIM_C ** -0.5)
    return pl.pallas_call(
        kern,
        grid=(nb,),
        in_specs=[pl.BlockSpec((1, nkv, ngt, d), lambda b: (b, 0, 0, 0)),
                  pl.BlockSpec((1, nkv, keys, d), lambda b: (b, 0, 0, 0)),
                  pl.BlockSpec((1, nkv, keys, d), lambda b: (b, 0, 0, 0)),
                  pl.BlockSpec((nkv, ngt, keys), lambda b: (0, 0, 0)),
                  pl.BlockSpec((nkv, ngt, LANES), lambda b: (0, 0, 0))],
        out_specs=pl.BlockSpec((1, nkv, ngt, d), lambda b: (b, 0, 0, 0)),
        out_shape=jax.ShapeDtypeStruct((nb, nkv, ngt, d), BF16),
        compiler_params=_cparams(("parallel",), 32),
        name="swa_sample",
    )(q_r, k_r, v_r, bias, sink_rows)


def _lower_bound(lbp_ref, layer):
    x = lbp_ref[...]
    e = jnp.exp(x - jnp.max(x, axis=0, keepdims=True))
    den = jnp.sum(e, axis=0, keepdims=True)
    num = e[1:2]
    for l in range(2, layer + 1):
        num = num + e[l:l + 1]
    if layer == 0:
        num = jnp.zeros_like(den)
    return num / den


def _cumsum_rows(x):
    rows = x.shape[0]
    row = lax.broadcasted_iota(jnp.int32, x.shape, 0)
    sh = 1
    while sh < rows:
        x = x + jnp.where(row >= sh, pltpu.roll(x, sh, axis=0), 0.0)
        sh *= 2
    return x


def _gla_gates(qr, fr, lb):
    q = qr * _sigmoid(qr)
    f = lb + (1.0 - lb) * _sigmoid(fr)
    return q, 1.0 - f, jnp.log2(f)


def _gla_scores(q, k, bcum, sub):
    rows = q.shape[0]
    sl = 8
    srow = lax.broadcasted_iota(jnp.int32, (sl, LANES), 0)
    lane = lax.broadcasted_iota(jnp.int32, (sl, LANES), 1)
    out = []
    for i in range(rows // sub):
        lo = i * sub
        qi, ki, bi = q[lo:lo + sub], k[lo:lo + sub], bcum[lo:lo + sub]
        a = jnp.zeros((sub, LANES), F32)
        if i > 0:
            ref = bcum[lo - 1:lo]
            qd = qi * jnp.exp2(bi - ref)
            kd = jnp.concatenate([k[:lo] * jnp.exp2(ref - bcum[:lo]), jnp.zeros((rows - lo, LANES), F32)], axis=0)
            a = _dot_nt(qd.astype(BF16), kd.astype(BF16))
            if rows < LANES:
                a = jnp.concatenate([a, jnp.zeros((sub, LANES - rows), F32)], axis=1)
        slabs = [a[r:r + sl] for r in range(0, sub, sl)]
        for s in range(sub):
            for n in range(s // sl, sub // sl):
                r = n * sl
                d = bi[r:r + sl] - bi[s:s + 1]
                if n == s // sl:
                    d = jnp.where(srow + r >= s, d, -jnp.inf)
                col = jnp.sum(qi[r:r + sl] * jnp.exp2(d) * ki[s:s + 1], axis=-1, keepdims=True)
                slabs[n] = jnp.where(lane == lo + s, col, slabs[n])
        out.extend(slabs)
    return out[0] if len(out) == 1 else jnp.concatenate(out, axis=0)


def _pad_rows(x, rows):
    if x.shape[0] == rows:
        return x
    return jnp.concatenate([x, jnp.zeros((rows - x.shape[0], x.shape[1]), x.dtype)], axis=0)


def _gla_chunk(st, q, k, v, g, sub, last):
    bcum = _cumsum_rows(g)
    b_last = bcum[last:last + 1]
    o = _dot_nt((q * jnp.exp2(bcum)).astype(BF16), st.astype(BF16))
    a = _gla_scores(q, k, bcum, sub)
    v128 = _pad_rows(v, LANES).astype(BF16)
    o = o + _dot(a.astype(BF16), v128)
    kdec = _pad_rows(k * jnp.exp2(b_last - bcum), LANES).astype(BF16)
    vt = _pad_rows(v, LANES).T.astype(BF16)
    st_new = st * jnp.exp2(b_last) + _dot(vt, kdec)
    return o, st_new


def _gla_finish(o, gate, ng_ref):
    return (_rms(o, ng_ref[...]) * (gate * _sigmoid(gate))).astype(BF16)


def _hgrn_prompt_kernel(lbp_ref, hq_ref, hf_ref, hi_ref, hg_ref, ng_ref, y_ref, sfin_ref, st_scr, *, layer):
    r = pl.program_id(2)
    c = CHUNK_B

    @pl.when(r == 0)
    def _():
        st_scr[...] = jnp.zeros(st_scr.shape, F32)

    nhs = st_scr.shape[0]
    lbs = [_lower_bound(lbp_ref.at[:, hd * LANES:(hd + 1) * LANES], layer) for hd in range(nhs)]

    def body(ci, carry):
        r0 = pl.multiple_of(ci * c, c)
        rows = pl.ds(r0, c)
        for hd in range(nhs):
            cols = slice(hd * LANES, (hd + 1) * LANES)
            q, k, g = _gla_gates(hq_ref[rows, cols], hf_ref[rows, cols], lbs[hd])
            o, st_new = _gla_chunk(st_scr[hd], q, k, hi_ref[rows, cols], g, SUB_B, c - 1)
            st_scr[hd] = st_new
            y_ref[rows, cols] = _gla_finish(o, hg_ref[rows, cols], ng_ref)
        return carry

    lax.fori_loop(0, hq_ref.shape[0] // c, body, 0)

    @pl.when(r == pl.num_programs(2) - 1)
    def _():
        for hd in range(nhs):
            sfin_ref[0, hd] = st_scr[hd].T


def hgrn_prompt(h_in, nb, seq, lb_param, norm_g, layer):
    nh = h_in.shape[1] // (4 * LANES)
    nhs = HEADS_B
    assert nh % nhs == 0
    ng = nh // nhs
    rows = min(ROWS_B, seq)
    nr = seq // rows
    depth = lb_param.shape[0]
    kern = functools.partial(_hgrn_prompt_kernel, layer=layer)

    def col(j):
        return pl.BlockSpec((rows, nhs * LANES), lambda b, h, r: (b * nr + r, j * ng + h))

    return pl.pallas_call(
        kern,
        grid=(nb, ng, nr),
        in_specs=[pl.BlockSpec((depth, nhs * LANES), lambda b, h, r: (0, h)),
                  col(0), col(1), col(2), col(3),
                  pl.BlockSpec((1, LANES), lambda b, h, r: (0, 0))],
        out_specs=[pl.BlockSpec((rows, nhs * LANES), lambda b, h, r: (b * nr + r, h)),
                   pl.BlockSpec((1, nhs, LANES, LANES), lambda b, h, r: (b, h, 0, 0))],
        out_shape=[jax.ShapeDtypeStruct((nb * seq, nh * LANES), BF16),
                   jax.ShapeDtypeStruct((nb, nh, LANES, LANES), F32)],
        scratch_shapes=[pltpu.VMEM((nhs, LANES, LANES), F32)],
        compiler_params=_cparams(("parallel", "parallel", "arbitrary"), 32),
        name="hgrn_prompt",
    )(lb_param, h_in, h_in, h_in, h_in, norm_g.reshape(1, LANES))


def _hgrn_sample_kernel(lbp_ref, h_ref, ng_ref, s0_ref, y_ref, s1_ref, *, layer, tdec):
    nh = s0_ref.shape[1]
    rows = h_ref.shape[1]
    for hd in range(nh):
        def cols(j, hd=hd):
            return slice((j * nh + hd) * LANES, (j * nh + hd + 1) * LANES)
        lb = _lower_bound(lbp_ref.at[:, cols(0)], layer)
        q, k, g = _gla_gates(h_ref[0, :, cols(0)], h_ref[0, :, cols(1)], lb)
        o, st_new = _gla_chunk(s0_ref[0, hd].T, q, k, h_ref[0, :, cols(2)], g, rows, tdec - 1)
        y_ref[0, :, cols(0)] = _gla_finish(o, h_ref[0, :, cols(3)], ng_ref)
        s1_ref[0, hd] = st_new.T


def hgrn_sample(h_in, state, state_layer, lb_param, norm_g, layer, tdec):
    nb, rows, width = h_in.shape
    nh = width // (4 * LANES)
    depth = lb_param.shape[0]
    kern = functools.partial(_hgrn_sample_kernel, layer=layer, tdec=tdec)
    return pl.pallas_call(
        kern,
        grid=(nb,),
        in_specs=[pl.BlockSpec((depth, nh * LANES), lambda b: (0, 0)),
                  pl.BlockSpec((1, rows, width), lambda b: (b, 0, 0)),
                  pl.BlockSpec((1, LANES), lambda b: (0, 0)),
                  pl.BlockSpec((None, 1, nh, LANES, LANES), lambda b: (state_layer, b, 0, 0, 0))],
        out_specs=[pl.BlockSpec((1, rows, nh * LANES), lambda b: (b, 0, 0)),
                   pl.BlockSpec((1, nh, LANES, LANES), lambda b: (b, 0, 0, 0))],
        out_shape=[jax.ShapeDtypeStruct((nb, rows, nh * LANES), BF16),
                   jax.ShapeDtypeStruct((nb, nh, LANES, LANES), F32)],
        compiler_params=_cparams(("parallel",), 32),
        name="hgrn_sample",
    )(lb_param, h_in, norm_g.reshape(1, LANES), state)


def _row_tile(m, cands=(640, 512, 256, 128, 64, 32, 16, 8)):
    for t in cands:
        if m % t == 0:
            return t
    raise ValueError(f"unsupported row count {m}")


def _col_tile(n, cands):
    for t in cands:
        if n % t == 0:
            return t
    raise ValueError(f"unsupported column count {n}")


def kernel(x_prompt, x_sample, cache_kv_a, state_hgrn_b, cache_win_c, page_table, norm_g, a_w_in, a_w_out, a_lambda, a_subln_g, b_w_in, b_w_out, b_lower_bound, b_norm_g, c_w_in, c_w_out, c_sinks, rel_bias_table, ffn_w_gate_up, ffn_w_down):
    nb, seq, d = x_prompt.shape
    nbs, tdec, _ = x_sample.shape
    depth = norm_g.shape[0]
    mp, ms = nb * seq, nbs * tdec
    m = mp + ms
    tm = _row_tile(m)
    tm_in = _row_tile(m, (1040, 640, 512, 256, 128, 64, 32, 16, 8))
    straddle = ms < tm and mp % tm == tm - ms
    page = cache_kv_a.shape[2]
    assert page == LANES and page >= MAX_DISTANCE and tdec <= 8
    kvw = N_KV_A * 2 * HEAD_DIM_A
    cache = cache_kv_a.reshape(cache_kv_a.shape[0], cache_kv_a.shape[1], page * 2 * N_KV_A, 2 * HEAD_DIM_A)
    table = rel_bias_table.astype(F32)
    ga, gc = GROUP_A, GROUP_C
    wb = cache_win_c.shape[2]
    keys_c = 2 * WINDOW

    dist_a = _band_dist(TQ_A)
    bias_a = bias_tiles(table, dist_a, dist_a >= 0, unit=LOG2E)
    dist_c = _band_dist(WINDOW)
    bias_c = bias_tiles(table, dist_c, (dist_c >= 0) & (dist_c <= WINDOW))

    dist_as, in_as = _decode_dist(tdec, page, 2 * LANES)
    bias_as = bias_tiles(table, dist_as, in_as & (dist_as >= 0))
    bias_as = bias_as.reshape(N_KV_A, ga, 2, 8, 2 * LANES).transpose(0, 2, 1, 3, 4)[:, :, :, :tdec]
    bias_as = bias_as.reshape(N_KV_A, 2 * ga * tdec, 2 * LANES)
    last_a, new_a = bias_as[:, :, :LANES], bias_as[:, :, LANES:]
    far_a = table[NUM_BUCKETS - 1].reshape(N_KV_A, ga, 2).transpose(0, 2, 1)
    far_a = jnp.broadcast_to(far_a[:, :, :, None, None], (N_KV_A, 2, ga, tdec, LANES))
    far_a = far_a.reshape(N_KV_A, 2 * ga * tdec, LANES)
    dist_cs, in_cs = _decode_dist(tdec, wb, keys_c)
    bias_cs = bias_tiles(table, dist_cs, in_cs & (dist_cs >= 0) & (dist_cs <= WINDOW))
    bias_cs = bias_cs.reshape(N_KV_C, gc, 8, keys_c)[:, :, :tdec].reshape(N_KV_C, gc * tdec, keys_c)

    a_w_in16, a_w_out16 = a_w_in.astype(BF16), a_w_out.astype(BF16)
    b_w_in16, b_w_out16 = b_w_in.astype(BF16), b_w_out.astype(BF16)
    c_w_in16, c_w_out16 = c_w_in.astype(BF16), c_w_out.astype(BF16)
    ffn_w_down16 = ffn_w_down.astype(BF16)
    dff = ffn_w_gate_up.shape[2] // 2

    h = jnp.concatenate([x_prompt.reshape(mp, d), x_sample.reshape(ms, d)], axis=0)
    kv_p, kv_s, hg_p, hg_s, win_p, win_s = [], [], [], [], [], []
    for layer in range(depth):
        kind, j = layer % 3, layer // 3
        g = norm_g[layer]
        if kind == 0:
            lam_init = 0.8 - 0.6 * math.exp(-0.3 * layer)
            nslot = 2 * kvw // LANES
            qkv, kv_rows = norm_matmul(h, g[0], a_w_in16, j, tm=tm_in, tn=2 * kvw, tail_rows=True)
            dq = N_KV_A * GROUP_A * 2 * HEAD_DIM_A
            lam_q, lam_k = a_lambda[j][0::2], a_lambda[j][1::2]
            yp = attn_a_prompt(qkv, nb, seq, table, bias_a, lam_q, lam_k, a_subln_g[j], lam_init)
            qs = qkv[mp:, :dq].reshape(nbs, tdec, N_KV_A, ga, LANES).transpose(0, 2, 3, 1, 4)
            qs = qs.reshape(nbs, N_KV_A, ga * tdec, LANES)
            kvs = qkv[mp:, dq:].reshape(nbs, tdec, 2 * kvw)
            ys = attn_a_sample(qs, cache, j, page_table, kvs, far_a, last_a, new_a,
                               lam_q, lam_k, a_subln_g[j], lam_init)
            ys = ys.reshape(nbs, N_KV_A, ga, tdec, LANES).transpose(0, 3, 1, 2, 4).reshape(ms, dq)
            kv_p.append(kv_rows[:mp * nslot].reshape(nb, seq, 2, N_KV_A, 2 * HEAD_DIM_A))
            kv_s.append(kv_rows[mp * nslot:].reshape(nbs, tdec, 2, N_KV_A, 2 * HEAD_DIM_A))
            w_out = a_w_out16
        elif kind == 1:
            hin = norm_matmul(h, g[0], b_w_in16, j, tm=tm_in, tn=_col_tile(b_w_in.shape[2], (1024, 512, 256, 128)))
            yp, sp = hgrn_prompt(hin, nb, seq, b_lower_bound, b_norm_g[j], layer)
            hs_in = jnp.pad(hin[mp:].reshape(nbs, tdec, hin.shape[1]), ((0, 0), (0, 8 - tdec), (0, 0)))
            ys, ss = hgrn_sample(hs_in, state_hgrn_b, j, b_lower_bound, b_norm_g[j], layer, tdec)
            ys = ys[:, :tdec].reshape(ms, d)
            hg_p.append(sp)
            hg_s.append(ss)
            w_out = b_w_out16
        else:
            qkv = norm_matmul(h, g[0], c_w_in16, j, tm=tm_in, tn=_col_tile(c_w_in.shape[2], (1280, 512, 256, 128)))
            dq = N_KV_C * GROUP_C * HEAD_DIM_C
            dkv = N_KV_C * HEAD_DIM_C
            yp = swa_prompt(qkv, nb, seq, c_sinks[j], bias_c)
            kv_new = qkv[mp:, dq:].reshape(nbs, tdec, 2 * dkv)
            kv_all = jnp.concatenate([cache_win_c[j].reshape(nbs, wb, 2 * dkv), kv_new], axis=1)
            kv_pad = jnp.pad(kv_all, ((0, 0), (0, keys_c - wb - tdec), (0, 0)))
            kv_r = kv_pad.reshape(nbs, keys_c, 2, N_KV_C, HEAD_DIM_C).transpose(2, 0, 3, 1, 4)
            qs = qkv[mp:, :dq].reshape(nbs, tdec, N_KV_C, gc, HEAD_DIM_C).transpose(0, 2, 3, 1, 4)
            qs = qs.reshape(nbs, N_KV_C, gc * tdec, HEAD_DIM_C)
            sink_rows = jnp.broadcast_to(c_sinks[j].reshape(N_KV_C, gc, 1, 1), (N_KV_C, gc, tdec, LANES))
            sink_rows = sink_rows.reshape(N_KV_C, gc * tdec, LANES)
            ys = swa_sample(qs, kv_r[0], kv_r[1], bias_cs, sink_rows)
            ys = ys.reshape(nbs, N_KV_C, gc, tdec, HEAD_DIM_C).transpose(0, 3, 1, 2, 4).reshape(ms, dq)
            win_p.append(qkv[:mp, dq:].reshape(nb, seq, 2, N_KV_C, HEAD_DIM_C)[:, seq - wb:])
            win_s.append(kv_all[:, tdec:].reshape(nbs, wb, 2, N_KV_C, HEAD_DIM_C))
            w_out = c_w_out16
        if straddle:
            h = matmul_postnorm_residual(yp, w_out, j, g[1], h, tm=tm, y_tail=ys)
        else:
            h = matmul_postnorm_residual(jnp.concatenate([yp, ys], axis=0), w_out, j, g[1], h, tm=tm)
        hm = norm_swiglu(h, g[2], ffn_w_gate_up, layer, tm=tm_in, tn=_col_tile(dff, (512, 256, 128)))
        if straddle and layer == depth - 1:
            hp, hs = matmul_postnorm_residual(hm, ffn_w_down16, layer, g[3], h, tm=tm, split_lead=mp)
        else:
            h = matmul_postnorm_residual(hm, ffn_w_down16, layer, g[3], h, tm=tm)
            hp, hs = h[:mp], h[mp:]
    return (hp.reshape(nb, seq, d), hs.reshape(nbs, tdec, d),
            jnp.stack(kv_p), jnp.stack(kv_s), jnp.stack(hg_p), jnp.stack(hg_s),
            jnp.stack(win_p), jnp.stack(win_s))
```

```python
import functools
import math

import numpy as np
import jax
import jax.numpy as jnp
from jax import lax
from jax.experimental import pallas as pl
from jax.experimental.pallas import tpu as pltpu

F32 = jnp.float32
BF16 = jnp.bfloat16

NORM_EPS = 1e-6
NUM_BUCKETS = 32
MAX_DISTANCE = 128
NEG = -1e30
LANES = 128
HALF = LANES // 2
LOG2E = math.log2(math.e)
MIB = 1024 * 1024

HEAD_DIM_A = 64
N_KV_A = 4
GROUP_A = 4
HEAD_DIM_C = 64
N_KV_C = 4
GROUP_C = 8
WINDOW = 128
TQ_A = 256
PAGES_PER_STEP = 32
PAGE_GROUP = 32
CHUNK_B = 128
SUB_B = 16
ROWS_B = 512
HEADS_B = 4


def _cparams(sem, vmem_mib):
    return pltpu.CompilerParams(dimension_semantics=sem, vmem_limit_bytes=vmem_mib * MIB)


def _rms(x, g):
    ms = jnp.mean(x * x, axis=-1, keepdims=True)
    return x * lax.rsqrt(ms + NORM_EPS) * g


def _sigmoid(x):
    return 1.0 / (1.0 + jnp.exp(-x))


def _dot(a, b):
    return jnp.dot(a, b, preferred_element_type=F32)


def _dot_nt(a, b):
    return lax.dot_general(a, b, (((1,), (1,)), ((), ())), preferred_element_type=F32)


def _norm_matmul_kernel(x_ref, g_ref, w_ref, o_ref, *rest, n_tail):
    xn_ref = rest[-1]
    j = pl.program_id(1)

    @pl.when(j == 0)
    def _():
        xn_ref[...] = _rms(x_ref[...], g_ref[...]).astype(BF16)

    res = _dot(xn_ref[...], w_ref[...])
    o_ref[...] = res
    if n_tail:
        tail_ref = rest[0]
        tm = o_ref.shape[0]

        @pl.when(j == pl.num_programs(1) - 1)
        def _():
            for slot in range(n_tail):
                tail_ref[pl.ds(slot, tm, stride=n_tail), :] = res[:, slot * LANES:(slot + 1) * LANES]


def norm_matmul(x, g, w, layer, *, tm, tn, tail_rows=False):
    m, k = x.shape
    n = w.shape[2]
    assert m % tm == 0 and n % tn == 0
    n_tail = tn // LANES if tail_rows else 0
    out_specs = [pl.BlockSpec((tm, tn), lambda i, j: (i, j))]
    out_shape = [jax.ShapeDtypeStruct((m, n), F32)]
    if tail_rows:
        out_specs.append(pl.BlockSpec((tm * n_tail, LANES), lambda i, j: (i, 0)))
        out_shape.append(jax.ShapeDtypeStruct((m * n_tail, LANES), F32))
    out = pl.pallas_call(
        functools.partial(_norm_matmul_kernel, n_tail=n_tail),
        grid=(m // tm, n // tn),
        in_specs=[pl.BlockSpec((tm, k), lambda i, j: (i, 0)),
                  pl.BlockSpec((1, k), lambda i, j: (0, 0)),
                  pl.BlockSpec((None, k, tn), lambda i, j: (layer, 0, j))],
        out_specs=out_specs,
        out_shape=out_shape,
        scratch_shapes=[pltpu.VMEM((tm, k), BF16)],
        compiler_params=_cparams(("parallel", "arbitrary"), 56),
        name="norm_matmul",
    )(x, g.reshape(1, k), w)
    return out if tail_rows else out[0]


def _norm_swiglu_kernel(x_ref, g_ref, wg_ref, wu_ref, o_ref, xn_ref):
    @pl.when(pl.program_id(1) == 0)
    def _():
        xn_ref[...] = _rms(x_ref[...], g_ref[...]).astype(BF16)

    xn = xn_ref[...]
    gate = _dot(xn, wg_ref[...].astype(BF16))
    up = _dot(xn, wu_ref[...].astype(BF16))
    o_ref[...] = (gate * _sigmoid(gate) * up).astype(o_ref.dtype)


def norm_swiglu(x, g, w_gate_up, layer, *, tm, tn):
    m, k = x.shape
    dff = w_gate_up.shape[2] // 2
    assert m % tm == 0 and dff % tn == 0
    nj = dff // tn
    return pl.pallas_call(
        _norm_swiglu_kernel,
        grid=(m // tm, nj),
        in_specs=[pl.BlockSpec((tm, k), lambda i, j: (i, 0)),
                  pl.BlockSpec((1, k), lambda i, j: (0, 0)),
                  pl.BlockSpec((None, k, tn), lambda i, j: (layer, 0, j)),
                  pl.BlockSpec((None, k, tn), lambda i, j: (layer, 0, j + nj))],
        out_specs=pl.BlockSpec((tm, tn), lambda i, j: (i, j)),
        out_shape=jax.ShapeDtypeStruct((m, dff), BF16),
        scratch_shapes=[pltpu.VMEM((tm, k), BF16)],
        compiler_params=_cparams(("parallel", "arbitrary"), 56),
        name="norm_swiglu",
    )(x, g.reshape(1, k), w_gate_up, w_gate_up)


def _matmul_postnorm_kernel(*refs, nk, off, has_tail, split_out):
    refs = list(refs)
    y_ref = refs.pop(0)
    yt_ref = refs.pop(0) if has_tail else None
    w_ref, g_ref, h_ref, o_ref = refs[:4]
    ot_ref = refs[4] if split_out else None
    acc_ref = refs[-1]
    k = pl.program_id(1)
    last_tile = pl.program_id(0) == pl.num_programs(0) - 1

    def body(last):
        def part():
            if last and has_tail:
                y = jnp.concatenate([y_ref[:off, :], yt_ref[...]], axis=0)
            else:
                y = y_ref[...]
            return _dot(y, w_ref[...])

        def finish(acc):
            res = h_ref[...] + _rms(acc, g_ref[...])
            o_ref[...] = res
            if last and split_out:
                ot_ref[...] = res[off:]

        if nk == 1:
            finish(part())
            return

        @pl.when(k == 0)
        def _():
            acc_ref[...] = part()

        @pl.when(jnp.logical_and(k > 0, k < nk - 1))
        def _():
            acc_ref[...] += part()

        @pl.when(k == nk - 1)
        def _():
            finish(acc_ref[...] + part())

    if not (has_tail or split_out):
        body(False)
        return
    pl.when(jnp.logical_not(last_tile))(lambda: body(False))
    pl.when(last_tile)(lambda: body(True))


def matmul_postnorm_residual(y, w, layer, g, h, *, tm, y_tail=None, split_lead=None):
    m, n = h.shape
    kdim = y.shape[1]
    tk = kdim if kdim <= 2048 else _col_tile(kdim, (1408, 1024, 512, 256, 128))
    assert m % tm == 0 and kdim % tk == 0
    nk = kdim // tk
    ni = m // tm
    has_tail = y_tail is not None
    split_out = split_lead is not None
    lead = y.shape[0] if has_tail else (split_lead if split_out else m)
    off = lead - (ni - 1) * tm
    if has_tail or split_out:
        assert 0 < off < tm
        assert not has_tail or (y_tail.shape[0] == tm - off and lead + y_tail.shape[0] == m)
        assert not split_out or split_lead == lead
    in_specs = [pl.BlockSpec((tm, tk), lambda i, k: (i, k))]
    args = [y]
    if has_tail:
        in_specs.append(pl.BlockSpec((tm - off, tk), lambda i, k: (0, k)))
        args.append(y_tail)
    in_specs += [pl.BlockSpec((None, tk, n), lambda i, k: (layer, k, 0)),
                 pl.BlockSpec((1, n), lambda i, k: (0, 0)),
                 pl.BlockSpec((tm, n), lambda i, k: (i, 0))]
    args += [w, g.reshape(1, n), h]
    if split_out:
        out_specs = [pl.BlockSpec((tm, n), lambda i, k: (i, 0)), pl.BlockSpec((tm - off, n), lambda i, k: (0, 0))]
        out_shape = [jax.ShapeDtypeStruct((lead, n), F32), jax.ShapeDtypeStruct((tm - off, n), F32)]
    else:
        out_specs = pl.BlockSpec((tm, n), lambda i, k: (i, 0))
        out_shape = jax.ShapeDtypeStruct((m, n), F32)
    return pl.pallas_call(
        functools.partial(_matmul_postnorm_kernel, nk=nk, off=off, has_tail=has_tail, split_out=split_out),
        grid=(ni, nk),
        in_specs=in_specs,
        out_specs=out_specs,
        out_shape=out_shape,
        scratch_shapes=[pltpu.VMEM((tm, n) if nk > 1 else (8, LANES), F32)],
        compiler_params=_cparams(("parallel", "arbitrary"), 56),
        name="matmul_postnorm_residual",
    )(*args)


def _bucket_np(dist):
    n = np.maximum(dist, 0)
    max_exact = NUM_BUCKETS // 2
    ratio = np.log(np.maximum(n, 1).astype(np.float32) / np.float32(max_exact)) / np.float32(
        math.log(MAX_DISTANCE / max_exact))
    large = np.minimum(max_exact + (ratio * (NUM_BUCKETS - max_exact)).astype(np.int32), NUM_BUCKETS - 1)
    return np.where(n < max_exact, n, large).astype(np.int32)


def _bias_tile_kernel(tab_ref, bucket_ref, mask_ref, o_ref, *, unit):
    c = pl.program_id(0)
    b = bucket_ref[...]
    acc = jnp.zeros(b.shape, F32)
    for k in range(NUM_BUCKETS):
        acc = jnp.where(b == k, tab_ref[k, c] * unit, acc)
    o_ref[0] = jnp.where(mask_ref[...] > 0, acc, NEG)


def bias_tiles(table, dist, valid, unit=1.0):
    ncol = table.shape[1]
    r, c = dist.shape
    return pl.pallas_call(
        functools.partial(_bias_tile_kernel, unit=unit),
        grid=(ncol,),
        in_specs=[pl.BlockSpec(memory_space=pltpu.SMEM),
                  pl.BlockSpec((r, c), lambda i: (0, 0)),
                  pl.BlockSpec((r, c), lambda i: (0, 0))],
        out_specs=pl.BlockSpec((1, r, c), lambda i: (i, 0, 0)),
        out_shape=jax.ShapeDtypeStruct((ncol, r, c), F32),
        compiler_params=_cparams(("parallel",), 32),
        name="bias_tiles",
    )(table, jnp.asarray(_bucket_np(dist)), jnp.asarray(valid.astype(np.int32)))


def _band_dist(t):
    return np.arange(t)[:, None] + t - np.arange(2 * t)[None, :]


def _decode_dist(tdec, n_past, cols):
    t = np.arange(8)[:, None]
    k = np.arange(cols)[None, :]
    dist = np.where(k < n_past, t + n_past - k, t - (k - n_past))
    inside = (k < n_past + tdec) & (t < tdec)
    return dist, inside


def _diff_lambda(lq_ref, lk_ref, lam_init):
    e = jnp.exp(jnp.sum(lq_ref[...] * lk_ref[...], axis=-1, keepdims=True))
    return e[0:1] - e[1:2] + lam_init


def _split_maps(q):
    lane = lax.broadcasted_iota(jnp.int32, q.shape, 1)
    return jnp.where(lane < HALF, q, 0.0), jnp.where(lane >= HALF, q, 0.0)


def _attn_a_prompt_kernel(tab_ref, lq_ref, lk_ref, q_ref, k_ref, v_ref, bias_ref, sg_ref, o_ref,
                          k_scr, v_scr, s_scr, m_scr, acc_scr, *, lam_init, scale):
    kvh = pl.program_id(0)
    qi = pl.program_id(2)
    tq = q_ref.shape[0]
    nt = k_scr.shape[0]
    ng = GROUP_A
    nrg = 2 * ng

    @pl.when(qi == 0)
    def _():
        ones = jnp.ones((tq, LANES), BF16)
        for t in range(nt):
            k_scr[t] = k_ref[t * tq:(t + 1) * tq, :].astype(BF16)
            v_scr[t] = jnp.concatenate([v_ref[t * tq:(t + 1) * tq, :].astype(BF16), ones], axis=1)

    q = q_ref[...] * (scale * LOG2E)
    parts1, parts2 = [], []
    for g in range(ng):
        q1, q2 = _split_maps(q[:, g * LANES:(g + 1) * LANES])
        parts1.append(q1)
        parts2.append(q2)
    qs = jnp.concatenate(parts1 + parts2, axis=0).astype(BF16)

    def score_tile(t, bias_of_group, first=False):
        s = _dot_nt(qs, k_scr[t])
        for rg in range(nrg):
            rows = slice(rg * tq, (rg + 1) * tq)
            sg = s[rows] + bias_of_group(rg)
            s_scr[t, rows, :] = sg
            mx = sg[:, :LANES]
            for c in range(1, tq // LANES):
                mx = jnp.maximum(mx, sg[:, c * LANES:(c + 1) * LANES])
            m_scr[rows, :] = mx if first else jnp.maximum(m_scr[rows, :], mx)

    def col_of_group(rg):
        m, g = divmod(rg, ng)
        return 2 * g + m

    score_tile(qi, lambda rg: bias_ref[col_of_group(rg), :, tq:2 * tq], first=True)

    def far_body(t, carry):
        score_tile(t, lambda rg: tab_ref[NUM_BUCKETS - 1, kvh * nrg + col_of_group(rg)] * LOG2E)
        return carry

    lax.fori_loop(0, qi - 1, far_body, 0)

    @pl.when(qi >= 1)
    def _():
        score_tile(qi - 1, lambda rg: bias_ref[col_of_group(rg), :, 0:tq])

    m_row = jnp.max(m_scr[...], axis=-1, keepdims=True)

    def pv_tile(t, first=False):
        p = jnp.exp2(s_scr[t] - m_row)
        pv = _dot(p.astype(BF16), v_scr[t])
        if first:
            acc_scr[...] = pv
        else:
            acc_scr[...] += pv

    pv_tile(qi, first=True)

    def pv_body(t, carry):
        pv_tile(t)
        return carry

    lax.fori_loop(0, qi, pv_body, 0)

    acc = acc_scr[...]
    o_all = acc[:, :LANES] / acc[:, LANES:]
    half = ng * tq
    lam = _diff_lambda(lq_ref, lk_ref, lam_init)
    o = o_all[:half] - lam * o_all[half:]
    for g in range(ng):
        og = _rms(o[g * tq:(g + 1) * tq], sg_ref[...]) * (1.0 - lam_init)
        o_ref[:, g * LANES:(g + 1) * LANES] = og.astype(o_ref.dtype)


def attn_a_prompt(qkv, nb, seq, table, bias_near, lam_q, lam_k, subln_g, lam_init):
    tq = TQ_A
    nq = seq // tq
    nqb = (N_KV_A * GROUP_A * LANES) // LANES
    kern = functools.partial(_attn_a_prompt_kernel, lam_init=lam_init, scale=HEAD_DIM_A ** -0.5)
    nrow = 2 * GROUP_A * tq
    return pl.pallas_call(
        kern,
        grid=(N_KV_A, nb, nq),
        in_specs=[pl.BlockSpec(memory_space=pltpu.SMEM),
                  pl.BlockSpec((2, HEAD_DIM_A), lambda h, b, i: (0, 0)),
                  pl.BlockSpec((2, HEAD_DIM_A), lambda h, b, i: (0, 0)),
                  pl.BlockSpec((tq, GROUP_A * LANES), lambda h, b, i: (b * nq + i, h)),
                  pl.BlockSpec((seq, LANES), lambda h, b, i: (b, nqb + h)),
                  pl.BlockSpec((seq, LANES), lambda h, b, i: (b, nqb + N_KV_A + h)),
                  pl.BlockSpec((2 * GROUP_A, tq, 2 * tq), lambda h, b, i: (h, 0, 0)),
                  pl.BlockSpec((1, LANES), lambda h, b, i: (0, 0))],
        out_specs=pl.BlockSpec((tq, GROUP_A * LANES), lambda h, b, i: (b * nq + i, h)),
        out_shape=jax.ShapeDtypeStruct((nb * seq, N_KV_A * GROUP_A * LANES), BF16),
        scratch_shapes=[pltpu.VMEM((nq, tq, LANES), BF16),
                        pltpu.VMEM((nq, tq, 2 * LANES), BF16),
                        pltpu.VMEM((nq, nrow, tq), F32),
                        pltpu.VMEM((nrow, LANES), F32),
                        pltpu.VMEM((nrow, 2 * LANES), F32)],
        compiler_params=_cparams(("parallel", "parallel", "arbitrary"), 56),
        name="attn_a_prompt",
    )(table, lam_q, lam_k, qkv, qkv, qkv, bias_near, subln_g.reshape(1, LANES))


def _attn_a_sample_kernel(pt_ref, lq_ref, lk_ref, q_ref, *rest, lam_init, scale, npp):
    page_refs = rest[:npp]
    (knew_ref, bfar_ref, blast_ref, bnew_ref, sg_ref, o_ref, m_scr, l_scr, acc_scr) = rest[npp:]
    ps = pl.program_id(1)
    nsteps = pl.num_programs(1)
    nkv = N_KV_A
    tdec = knew_ref.shape[1]

    @pl.when(ps == 0)
    def _():
        m_scr[...] = jnp.full(m_scr.shape, -jnp.inf, F32)
        l_scr[...] = jnp.zeros(l_scr.shape, F32)
        acc_scr[...] = jnp.zeros(acc_scr.shape, F32)

    qs32 = []
    for h in range(nkv):
        q1, q2 = _split_maps(q_ref[0, h] * scale)
        qs32.append(jnp.concatenate([q1, q2], axis=0))
    qs = [x.astype(BF16) for x in qs32]
    is_last = ps == nsteps - 1

    nslot = 2 * nkv
    page = page_refs[0].shape[0] // nslot

    def slot_rows(slot, refs):
        rows = [r[pl.ds(slot, page, stride=nslot), :] for r in refs]
        return jnp.concatenate(rows, axis=0).astype(BF16)

    grp = min(PAGE_GROUP, npp)
    for g0 in range(0, npp, grp):
        refs = page_refs[g0:g0 + grp]
        ps_list, alphas = [], []
        for h in range(nkv):
            kh = slot_rows(h, refs)
            s = _dot_nt(qs[h], kh)
            bfar = bfar_ref[h]
            if g0 + grp == npp:
                tail = jnp.where(is_last, blast_ref[h], bfar)
                s = jnp.concatenate([s[:, :-page] + bfar[:, 0:1], s[:, -page:] + tail], axis=1)
            else:
                s = s + bfar[:, 0:1]
            m_prev = m_scr[h]
            m_new = jnp.maximum(m_prev, jnp.max(s, axis=-1, keepdims=True))
            alpha = jnp.exp(m_prev - m_new)
            p = jnp.exp(s - m_new[:, 0:1])
            l_scr[h] = alpha * l_scr[h] + jnp.sum(p, axis=-1, keepdims=True)
            m_scr[h] = m_new
            ps_list.append(p.astype(BF16))
            alphas.append(alpha)
        for h in range(nkv):
            vh = slot_rows(nkv + h, refs)
            acc_scr[h] = alphas[h] * acc_scr[h] + _dot(ps_list[h], vh)

    @pl.when(is_last)
    def _():
        lam = _diff_lambda(lq_ref, lk_ref, lam_init)
        knew = knew_ref[0]
        for h in range(nkv):
            kn = knew[:, h * LANES:(h + 1) * LANES]
            vn = knew[:, (nkv + h) * LANES:(nkv + h + 1) * LANES]
            bnew = bnew_ref[h]
            cols = [jnp.sum(qs32[h] * kn[t:t + 1, :], axis=-1, keepdims=True) + bnew[:, t:t + 1]
                    for t in range(tdec)]
            m_prev = m_scr[h]
            m_new = m_prev
            for c in cols:
                m_new = jnp.maximum(m_new, c)
            alpha = jnp.exp(m_prev - m_new)
            l = alpha * l_scr[h]
            acc = alpha * acc_scr[h]
            for t in range(tdec):
                p = jnp.exp(cols[t] - m_new)
                l = l + p
                acc = acc + p * vn[t:t + 1, :]
            o_all = acc / l
            half = o_all.shape[0] // 2
            o = o_all[:half] - lam * o_all[half:]
            o_ref[0, h] = (_rms(o, sg_ref[...]) * (1.0 - lam_init)).astype(o_ref.dtype)


def attn_a_sample(q_r, cache, layer, page_table, knew, bias_far, bias_last, bias_new,
                  lam_q, lam_k, subln_g, lam_init):
    nb, nkv, ngt, _ = q_r.shape
    n_pages = page_table.shape[1]
    page_rows = cache.shape[2]
    assert page_rows == 2 * nkv * LANES and cache.shape[3] == LANES
    npp = min(PAGES_PER_STEP, n_pages)
    assert n_pages % npp == 0
    tdec = knew.shape[1]
    width = knew.shape[2]
    kern = functools.partial(_attn_a_sample_kernel, lam_init=lam_init, scale=HEAD_DIM_A ** -0.5, npp=npp)

    def page_spec(i):
        return pl.BlockSpec((None, None, page_rows, LANES),
                            lambda b, p, pt: (layer, pt[b, p * npp + i], 0, 0))

    const3 = lambda b, p, pt: (0, 0, 0)
    grid_spec = pltpu.PrefetchScalarGridSpec(
        num_scalar_prefetch=1,
        grid=(nb, n_pages // npp),
        in_specs=[pl.BlockSpec((2, HEAD_DIM_A), lambda b, p, pt: (0, 0)),
                  pl.BlockSpec((2, HEAD_DIM_A), lambda b, p, pt: (0, 0)),
                  pl.BlockSpec((1, nkv, ngt, LANES), lambda b, p, pt: (b, 0, 0, 0))]
                 + [page_spec(i) for i in range(npp)]
                 + [pl.BlockSpec((1, tdec, width), lambda b, p, pt: (b, 0, 0)),
                    pl.BlockSpec((nkv, 2 * ngt, LANES), const3),
                    pl.BlockSpec((nkv, 2 * ngt, LANES), const3),
                    pl.BlockSpec((nkv, 2 * ngt, LANES), const3),
                    pl.BlockSpec((1, LANES), lambda b, p, pt: (0, 0))],
        out_specs=pl.BlockSpec((1, nkv, ngt, LANES), lambda b, p, pt: (b, 0, 0, 0)),
        scratch_shapes=[pltpu.VMEM((nkv, 2 * ngt, LANES), F32),
                        pltpu.VMEM((nkv, 2 * ngt, LANES), F32),
                        pltpu.VMEM((nkv, 2 * ngt, LANES), F32)],
    )
    return pl.pallas_call(
        kern,
        grid_spec=grid_spec,
        out_shape=jax.ShapeDtypeStruct((nb, nkv, ngt, LANES), BF16),
        compiler_params=_cparams(("parallel", "arbitrary"), 56),
        name="attn_a_sample",
    )(page_table, lam_q, lam_k, q_r, *([cache] * npp), knew, bias_far, bias_last, bias_new,
      subln_g.reshape(1, LANES))


def _dup_half(x, odd):
    lane = lax.broadcasted_iota(jnp.int32, x.shape, 1)
    rolled = pltpu.roll(x, HALF, axis=1)
    keep = (lane >= HALF) if odd else (lane < HALF)
    return jnp.where(keep, x, rolled)


def _swa_prompt_kernel(sink_ref, q_ref, kvp_ref, kvo_ref, bias_ref, o_ref, *, scale):
    n = pl.program_id(1)
    w = q_ref.shape[0]
    ng = GROUP_C
    lane = lax.broadcasted_iota(jnp.int32, (w, LANES), 1)
    kcol = lax.broadcasted_iota(jnp.int32, (2 * w, 2 * w), 1)
    first = n == 0
    vpair0 = (N_KV_C * HEAD_DIM_C) // LANES
    ones = jnp.ones((2 * w, LANES), BF16)

    def kv_pair(c):
        cols = slice(c * LANES, (c + 1) * LANES)
        return jnp.concatenate([kvp_ref[:, cols], kvo_ref[:, cols]], axis=0)

    for h in range(N_KV_C):
        kd = _dup_half(kv_pair(h // 2), h % 2).astype(BF16)
        vd = _dup_half(kv_pair(vpair0 + h // 2), h % 2).astype(BF16)
        vd1 = jnp.concatenate([vd, ones], axis=1)
        for j in range(ng // 2):
            hd = h * ng + 2 * j
            col = (hd // 2) * LANES
            qp = q_ref[:, col:col + LANES] * scale
            qs = jnp.concatenate([jnp.where(lane < HALF, qp, 0.0), jnp.where(lane >= HALF, qp, 0.0)], axis=0)
            sink = jnp.concatenate([jnp.full((w, LANES), sink_ref[hd], F32),
                                    jnp.full((w, LANES), sink_ref[hd + 1], F32)], axis=0)
            bias = bias_ref[hd:hd + 2].reshape(2 * w, 2 * w)
            bias = jnp.where(jnp.logical_and(first, kcol < w), NEG, bias)
            s = _dot_nt(qs.astype(BF16), kd) + bias
            m = jnp.maximum(jnp.max(s, axis=-1, keepdims=True), sink)
            p = jnp.concatenate([jnp.exp(s[:, c * LANES:(c + 1) * LANES] - m) for c in range(2 * w // LANES)],
                                axis=1)
            ov = _dot(p.astype(BF16), vd1)
            o = ov[:, :LANES] / (ov[:, LANES:] + jnp.exp(sink - m))
            o_ref[:, col:col + LANES] = jnp.where(lane < HALF, o[:w], o[w:]).astype(o_ref.dtype)


def swa_prompt(qkv, nb, seq, sinks, bias_swa):
    w = WINDOW
    nblk = seq // w
    dq = N_KV_C * GROUP_C * HEAD_DIM_C
    dkv = 2 * N_KV_C * HEAD_DIM_C
    kern = functools.partial(_swa_prompt_kernel, scale=HEAD_DIM_C ** -0.5)
    return pl.pallas_call(
        kern,
        grid=(nb, nblk),
        in_specs=[pl.BlockSpec(memory_space=pltpu.SMEM),
                  pl.BlockSpec((w, dq), lambda b, n: (b * nblk + n, 0)),
                  pl.BlockSpec((w, dkv), lambda b, n: (b * nblk + jnp.maximum(n - 1, 0), dq // dkv)),
                  pl.BlockSpec((w, dkv), lambda b, n: (b * nblk + n, dq // dkv)),
                  pl.BlockSpec((N_KV_C * GROUP_C, w, 2 * w), lambda b, n: (0, 0, 0))],
        out_specs=pl.BlockSpec((w, dq), lambda b, n: (b * nblk + n, 0)),
        out_shape=jax.ShapeDtypeStruct((nb * seq, dq), BF16),
        compiler_params=_cparams(("parallel", "parallel"), 48),
        name="swa_prompt",
    )(sinks, qkv, qkv, qkv, bias_swa)


def _swa_sample_kernel(q_ref, k_ref, v_ref, bias_ref, sink_ref, o_ref, *, scale):
    for h in range(N_KV_C):
        q = (q_ref[0, h] * scale).astype(BF16)
        s = _dot_nt(q, k_ref[0, h].astype(BF16)) + bias_ref[h]
        sink = sink_ref[h][:, 0:1]
        m = jnp.maximum(jnp.max(s, axis=-1, keepdims=True), sink)
        p = jnp.exp(s - m)
        den = jnp.sum(p, axis=-1, keepdims=True) + jnp.exp(sink - m)
        o_ref[0, h] = (_dot(p.astype(BF16), v_ref[0, h].astype(BF16)) / den).astype(o_ref.dtype)


def swa_sample(q_r, k_r, v_r, bias, sink_rows):
    nb, nkv, ngt, d = q_r.shape
    keys = k_r.shape[2]
    kern = functools.partial(_swa_sample_kernel, scale=HEAD_DIM_C ** -0.5)
    return pl.pallas_call(
        kern,
        grid=(nb,),
        in_specs=[pl.BlockSpec((1, nkv, ngt, d), lambda b: (b, 0, 0, 0)),
                  pl.BlockSpec((1, nkv, keys, d), lambda b: (b, 0, 0, 0)),
                  pl.BlockSpec((1, nkv, keys, d), lambda b: (b, 0, 0, 0)),
                  pl.BlockSpec((nkv, ngt, keys), lambda b: (0, 0, 0)),
                  pl.BlockSpec((nkv, ngt, LANES), lambda b: (0, 0, 0))],
        out_specs=pl.BlockSpec((1, nkv, ngt, d), lambda b: (b, 0, 0, 0)),
        out_shape=jax.ShapeDtypeStruct((nb, nkv, ngt, d), BF16),
        compiler_params=_cparams(("parallel",), 32),
        name="swa_sample",
    )(q_r, k_r, v_r, bias, sink_rows)


def _lower_bound(lbp_ref, layer):
    x = lbp_ref[...]
    e = jnp.exp(x - jnp.max(x, axis=0, keepdims=True))
    den = jnp.sum(e, axis=0, keepdims=True)
    num = e[1:2]
    for l in range(2, layer + 1):
        num = num + e[l:l + 1]
    if layer == 0:
        num = jnp.zeros_like(den)
    return num / den


def _cumsum_rows(x):
    rows = x.shape[0]
    row = lax.broadcasted_iota(jnp.int32, x.shape, 0)
    sh = 1
    while sh < rows:
        x = x + jnp.where(row >= sh, pltpu.roll(x, sh, axis=0), 0.0)
        sh *= 2
    return x


def _gla_gates(qr, fr, lb):
    q = qr * _sigmoid(qr)
    f = lb + (1.0 - lb) * _sigmoid(fr)
    return q, 1.0 - f, jnp.log2(f)


def _gla_scores(q, k, bcum, sub):
    rows = q.shape[0]
    sl = 8
    srow = lax.broadcasted_iota(jnp.int32, (sl, LANES), 0)
    lane = lax.broadcasted_iota(jnp.int32, (sl, LANES), 1)
    out = []
    for i in range(rows // sub):
        lo = i * sub
        qi, ki, bi = q[lo:lo + sub], k[lo:lo + sub], bcum[lo:lo + sub]
        a = jnp.zeros((sub, LANES), F32)
        if i > 0:
            ref = bcum[lo - 1:lo]
            qd = qi * jnp.exp2(bi - ref)
            kd = jnp.concatenate([k[:lo] * jnp.exp2(ref - bcum[:lo]), jnp.zeros((rows - lo, LANES), F32)], axis=0)
            a = _dot_nt(qd.astype(BF16), kd.astype(BF16))
            if rows < LANES:
                a = jnp.concatenate([a, jnp.zeros((sub, LANES - rows), F32)], axis=1)
        slabs = [a[r:r + sl] for r in range(0, sub, sl)]
        for s in range(sub):
            for n in range(s // sl, sub // sl):
                r = n * sl
                d = bi[r:r + sl] - bi[s:s + 1]
                if n == s // sl:
                    d = jnp.where(srow + r >= s, d, -jnp.inf)
                col = jnp.sum(qi[r:r + sl] * jnp.exp2(d) * ki[s:s + 1], axis=-1, keepdims=True)
                slabs[n] = jnp.where(lane == lo + s, col, slabs[n])
        out.extend(slabs)
    return out[0] if len(out) == 1 else jnp.concatenate(out, axis=0)


def _pad_rows(x, rows):
    if x.shape[0] == rows:
        return x
    return jnp.concatenate([x, jnp.zeros((rows - x.shape[0], x.shape[1]), x.dtype)], axis=0)


def _gla_chunk(st, q, k, v, g, sub, last):
    bcum = _cumsum_rows(g)
    b_last = bcum[last:last + 1]
    o = _dot_nt((q * jnp.exp2(bcum)).astype(BF16), st.astype(BF16))
    a = _gla_scores(q, k, bcum, sub)
    v128 = _pad_rows(v, LANES).astype(BF16)
    o = o + _dot(a.astype(BF16), v128)
    kdec = _pad_rows(k * jnp.exp2(b_last - bcum), LANES).astype(BF16)
    vt = _pad_rows(v, LANES).T.astype(BF16)
    st_new = st * jnp.exp2(b_last) + _dot(vt, kdec)
    return o, st_new


def _gla_finish(o, gate, ng_ref):
    return (_rms(o, ng_ref[...]) * (gate * _sigmoid(gate))).astype(BF16)


def _hgrn_prompt_kernel(lbp_ref, hq_ref, hf_ref, hi_ref, hg_ref, ng_ref, y_ref, sfin_ref, st_scr, *, layer):
    r = pl.program_id(2)
    c = CHUNK_B

    @pl.when(r == 0)
    def _():
        st_scr[...] = jnp.zeros(st_scr.shape, F32)

    nhs = st_scr.shape[0]
    lbs = [_lower_bound(lbp_ref.at[:, hd * LANES:(hd + 1) * LANES], layer) for hd in range(nhs)]

    def body(ci, carry):
        r0 = pl.multiple_of(ci * c, c)
        rows = pl.ds(r0, c)
        for hd in range(nhs):
            cols = slice(hd * LANES, (hd + 1) * LANES)
            q, k, g = _gla_gates(hq_ref[rows, cols], hf_ref[rows, cols], lbs[hd])
            o, st_new = _gla_chunk(st_scr[hd], q, k, hi_ref[rows, cols], g, SUB_B, c - 1)
            st_scr[hd] = st_new
            y_ref[rows, cols] = _gla_finish(o, hg_ref[rows, cols], ng_ref)
        return carry

    lax.fori_loop(0, hq_ref.shape[0] // c, body, 0)

    @pl.when(r == pl.num_programs(2) - 1)
    def _():
        for hd in range(nhs):
            sfin_ref[0, hd] = st_scr[hd].T


def hgrn_prompt(h_in, nb, seq, lb_param, norm_g, layer):
    nh = h_in.shape[1] // (4 * LANES)
    nhs = HEADS_B
    assert nh % nhs == 0
    ng = nh // nhs
    rows = min(ROWS_B, seq)
    nr = seq // rows
    depth = lb_param.shape[0]
    kern = functools.partial(_hgrn_prompt_kernel, layer=layer)

    def col(j):
        return pl.BlockSpec((rows, nhs * LANES), lambda b, h, r: (b * nr + r, j * ng + h))

    return pl.pallas_call(
        kern,
        grid=(nb, ng, nr),
        in_specs=[pl.BlockSpec((depth, nhs * LANES), lambda b, h, r: (0, h)),
                  col(0), col(1), col(2), col(3),
                  pl.BlockSpec((1, LANES), lambda b, h, r: (0, 0))],
        out_specs=[pl.BlockSpec((rows, nhs * LANES), lambda b, h, r: (b * nr + r, h)),
                   pl.BlockSpec((1, nhs, LANES, LANES), lambda b, h, r: (b, h, 0, 0))],
        out_shape=[jax.ShapeDtypeStruct((nb * seq, nh * LANES), BF16),
                   jax.ShapeDtypeStruct((nb, nh, LANES, LANES), F32)],
        scratch_shapes=[pltpu.VMEM((nhs, LANES, LANES), F32)],
        compiler_params=_cparams(("parallel", "parallel", "arbitrary"), 32),
        name="hgrn_prompt",
    )(lb_param, h_in, h_in, h_in, h_in, norm_g.reshape(1, LANES))


def _hgrn_sample_kernel(lbp_ref, h_ref, ng_ref, s0_ref, y_ref, s1_ref, *, layer, tdec):
    nh = s0_ref.shape[1]
    rows = h_ref.shape[1]
    for hd in range(nh):
        def cols(j, hd=hd):
            return slice((j * nh + hd) * LANES, (j * nh + hd + 1) * LANES)
        lb = _lower_bound(lbp_ref.at[:, cols(0)], layer)
        q, k, g = _gla_gates(h_ref[0, :, cols(0)], h_ref[0, :, cols(1)], lb)
        o, st_new = _gla_chunk(s0_ref[0, hd].T, q, k, h_ref[0, :, cols(2)], g, rows, tdec - 1)
        y_ref[0, :, cols(0)] = _gla_finish(o, h_ref[0, :, cols(3)], ng_ref)
        s1_ref[0, hd] = st_new.T


def hgrn_sample(h_in, state, state_layer, lb_param, norm_g, layer, tdec):
    nb, rows, width = h_in.shape
    nh = width // (4 * LANES)
    depth = lb_param.shape[0]
    kern = functools.partial(_hgrn_sample_kernel, layer=layer, tdec=tdec)
    return pl.pallas_call(
        kern,
        grid=(nb,),
        in_specs=[pl.BlockSpec((depth, nh * LANES), lambda b: (0, 0)),
                  pl.BlockSpec((1, rows, width), lambda b: (b, 0, 0)),
                  pl.BlockSpec((1, LANES), lambda b: (0, 0)),
                  pl.BlockSpec((None, 1, nh, LANES, LANES), lambda b: (state_layer, b, 0, 0, 0))],
        out_specs=[pl.BlockSpec((1, rows, nh * LANES), lambda b: (b, 0, 0)),
                   pl.BlockSpec((1, nh, LANES, LANES), lambda b: (b, 0, 0, 0))],
        out_shape=[jax.ShapeDtypeStruct((nb, rows, nh * LANES), BF16),
                   jax.ShapeDtypeStruct((nb, nh, LANES, LANES), F32)],
        compiler_params=_cparams(("parallel",), 32),
        name="hgrn_sample",
    )(lb_param, h_in, norm_g.reshape(1, LANES), state)


def _row_tile(m, cands=(640, 512, 256, 128, 64, 32, 16, 8)):
    for t in cands:
        if m % t == 0:
            return t
    raise ValueError(f"unsupported row count {m}")


def _col_tile(n, cands):
    for t in cands:
        if n % t == 0:
            return t
    raise ValueError(f"unsupported column count {n}")


def kernel(x_prompt, x_sample, cache_kv_a, state_hgrn_b, cache_win_c, page_table, norm_g, a_w_in, a_w_out, a_lambda, a_subln_g, b_w_in, b_w_out, b_lower_bound, b_norm_g, c_w_in, c_w_out, c_sinks, rel_bias_table, ffn_w_gate_up, ffn_w_down):
    nb, seq, d = x_prompt.shape
    nbs, tdec, _ = x_sample.shape
    depth = norm_g.shape[0]
    mp, ms = nb * seq, nbs * tdec
    m = mp + ms
    tm = _row_tile(m)
    tm_in = _row_tile(m, (1040, 640, 512, 256, 128, 64, 32, 16, 8))
    straddle = ms < tm and mp % tm == tm - ms
    page = cache_kv_a.shape[2]
    assert page == LANES and page >= MAX_DISTANCE and tdec <= 8
    kvw = N_KV_A * 2 * HEAD_DIM_A
    cache = cache_kv_a.reshape(cache_kv_a.shape[0], cache_kv_a.shape[1], page * 2 * N_KV_A, 2 * HEAD_DIM_A)
    table = rel_bias_table.astype(F32)
    ga, gc = GROUP_A, GROUP_C
    wb = cache_win_c.shape[2]
    keys_c = 2 * WINDOW

    dist_a = _band_dist(TQ_A)
    bias_a = bias_tiles(table, dist_a, dist_a >= 0, unit=LOG2E)
    dist_c = _band_dist(WINDOW)
    bias_c = bias_tiles(table, dist_c, (dist_c >= 0) & (dist_c <= WINDOW))

    dist_as, in_as = _decode_dist(tdec, page, 2 * LANES)
    bias_as = bias_tiles(table, dist_as, in_as & (dist_as >= 0))
    bias_as = bias_as.reshape(N_KV_A, ga, 2, 8, 2 * LANES).transpose(0, 2, 1, 3, 4)[:, :, :, :tdec]
    bias_as = bias_as.reshape(N_KV_A, 2 * ga * tdec, 2 * LANES)
    last_a, new_a = bias_as[:, :, :LANES], bias_as[:, :, LANES:]
    far_a = table[NUM_BUCKETS - 1].reshape(N_KV_A, ga, 2).transpose(0, 2, 1)
    far_a = jnp.broadcast_to(far_a[:, :, :, None, None], (N_KV_A, 2, ga, tdec, LANES))
    far_a = far_a.reshape(N_KV_A, 2 * ga * tdec, LANES)
    dist_cs, in_cs = _decode_dist(tdec, wb, keys_c)
    bias_cs = bias_tiles(table, dist_cs, in_cs & (dist_cs >= 0) & (dist_cs <= WINDOW))
    bias_cs = bias_cs.reshape(N_KV_C, gc, 8, keys_c)[:, :, :tdec].reshape(N_KV_C, gc * tdec, keys_c)

    a_w_in16, a_w_out16 = a_w_in.astype(BF16), a_w_out.astype(BF16)
    b_w_in16, b_w_out16 = b_w_in.astype(BF16), b_w_out.astype(BF16)
    c_w_in16, c_w_out16 = c_w_in.astype(BF16), c_w_out.astype(BF16)
    ffn_w_down16 = ffn_w_down.astype(BF16)
    dff = ffn_w_gate_up.shape[2] // 2

    h = jnp.concatenate([x_prompt.reshape(mp, d), x_sample.reshape(ms, d)], axis=0)
    kv_p, kv_s, hg_p, hg_s, win_p, win_s = [], [], [], [], [], []
    for layer in range(depth):
        kind, j = layer % 3, layer // 3
        g = norm_g[layer]
        if kind == 0:
            lam_init = 0.8 - 0.6 * math.exp(-0.3 * layer)
            nslot = 2 * kvw // LANES
            qkv, kv_rows = norm_matmul(h, g[0], a_w_in16, j, tm=tm_in, tn=2 * kvw, tail_rows=True)
            dq = N_KV_A * GROUP_A * 2 * HEAD_DIM_A
            lam_q, lam_k = a_lambda[j][0::2], a_lambda[j][1::2]
            yp = attn_a_prompt(qkv, nb, seq, table, bias_a, lam_q, lam_k, a_subln_g[j], lam_init)
            qs = qkv[mp:, :dq].reshape(nbs, tdec, N_KV_A, ga, LANES).transpose(0, 2, 3, 1, 4)
            qs = qs.reshape(nbs, N_KV_A, ga * tdec, LANES)
            kvs = qkv[mp:, dq:].reshape(nbs, tdec, 2 * kvw)
            ys = attn_a_sample(qs, cache, j, page_table, kvs, far_a, last_a, new_a,
                               lam_q, lam_k, a_subln_g[j], lam_init)
            ys = ys.reshape(nbs, N_KV_A, ga, tdec, LANES).transpose(0, 3, 1, 2, 4).reshape(ms, dq)
            kv_p.append(kv_rows[:mp * nslot].reshape(nb, seq, 2, N_KV_A, 2 * HEAD_DIM_A))
            kv_s.append(kv_rows[mp * nslot:].reshape(nbs, tdec, 2, N_KV_A, 2 * HEAD_DIM_A))
            w_out = a_w_out16
        elif kind == 1:
            hin = norm_matmul(h, g[0], b_w_in16, j, tm=tm_in, tn=_col_tile(b_w_in.shape[2], (1024, 512, 256, 128)))
            yp, sp = hgrn_prompt(hin, nb, seq, b_lower_bound, b_norm_g[j], layer)
            hs_in = jnp.pad(hin[mp:].reshape(nbs, tdec, hin.shape[1]), ((0, 0), (0, 8 - tdec), (0, 0)))
            ys, ss = hgrn_sample(hs_in, state_hgrn_b, j, b_lower_bound, b_norm_g[j], layer, tdec)
            ys = ys[:, :tdec].reshape(ms, d)
            hg_p.append(sp)
            hg_s.append(ss)
            w_out = b_w_out16
        else:
            qkv = norm_matmul(h, g[0], c_w_in16, j, tm=tm_in, tn=_col_tile(c_w_in.shape[2], (1280, 512, 256, 128)))
            dq = N_KV_C * GROUP_C * HEAD_DIM_C
            dkv = N_KV_C * HEAD_DIM_C
            yp = swa_prompt(qkv, nb, seq, c_sinks[j], bias_c)
            kv_new = qkv[mp:, dq:].reshape(nbs, tdec, 2 * dkv)
            kv_all = jnp.concatenate([cache_win_c[j].reshape(nbs, wb, 2 * dkv), kv_new], axis=1)
            kv_pad = jnp.pad(kv_all, ((0, 0), (0, keys_c - wb - tdec), (0, 0)))
            kv_r = kv_pad.reshape(nbs, keys_c, 2, N_KV_C, HEAD_DIM_C).transpose(2, 0, 3, 1, 4)
            qs = qkv[mp:, :dq].reshape(nbs, tdec, N_KV_C, gc, HEAD_DIM_C).transpose(0, 2, 3, 1, 4)
            qs = qs.reshape(nbs, N_KV_C, gc * tdec, HEAD_DIM_C)
            sink_rows = jnp.broadcast_to(c_sinks[j].reshape(N_KV_C, gc, 1, 1), (N_KV_C, gc, tdec, LANES))
            sink_rows = sink_rows.reshape(N_KV_C, gc * tdec, LANES)
            ys = swa_sample(qs, kv_r[0], kv_r[1], bias_cs, sink_rows)
            ys = ys.reshape(nbs, N_KV_C, gc, tdec, HEAD_DIM_C).transpose(0, 3, 1, 2, 4).reshape(ms, dq)
            win_p.append(qkv[:mp, dq:].reshape(nb, seq, 2, N_KV_C, HEAD_DIM_C)[:, seq - wb:])
            win_s.append(kv_all[:, tdec:].reshape(nbs, wb, 2, N_KV_C, HEAD_DIM_C))
            w_out = c_w_out16
        if straddle:
            h = matmul_postnorm_residual(yp, w_out, j, g[1], h, tm=tm, y_tail=ys)
        else:
            h = matmul_postnorm_residual(jnp.concatenate([yp, ys], axis=0), w_out, j, g[1], h, tm=tm)
        hm = norm_swiglu(h, g[2], ffn_w_gate_up, layer, tm=tm_in, tn=_col_tile(dff, (512, 256, 128)))
        if straddle and layer == depth - 1:
            hp, hs = matmul_postnorm_residual(hm, ffn_w_down16, layer, g[3], h, tm=tm, split_lead=mp)
        else:
            h = matmul_postnorm_residual(hm, ffn_w_down16, layer, g[3], h, tm=tm)
            hp, hs = h[:mp], h[mp:]
    return (hp.reshape(nb, seq, d), hs.reshape(nbs, tdec, d),
            jnp.stack(kv_p), jnp.stack(kv_s), jnp.stack(hg_p), jnp.stack(hg_s),
            jnp.stack(win_p), jnp.stack(win_s))
```

```python
import functools
import math

import numpy as np
import jax
import jax.numpy as jnp
from jax import lax
from jax.experimental import pallas as pl
from jax.experimental.pallas import tpu as pltpu

F32 = jnp.float32
BF16 = jnp.bfloat16

NORM_EPS = 1e-6
NUM_BUCKETS = 32
MAX_DISTANCE = 128
NEG = -1e30
LANES = 128
HALF = LANES // 2
LOG2E = math.log2(math.e)
MIB = 1024 * 1024

HEAD_DIM_A = 64
N_KV_A = 4
GROUP_A = 4
HEAD_DIM_C = 64
N_KV_C = 4
GROUP_C = 8
WINDOW = 128
TQ_A = 256
PAGES_PER_STEP = 32
PAGE_GROUP = 32
CHUNK_B = 128
SUB_B = 16
ROWS_B = 512
HEADS_B = 4


def _cparams(sem, vmem_mib):
    return pltpu.CompilerParams(dimension_semantics=sem, vmem_limit_bytes=vmem_mib * MIB)


def _rms(x, g):
    ms = jnp.mean(x * x, axis=-1, keepdims=True)
    return x * lax.rsqrt(ms + NORM_EPS) * g


def _sigmoid(x):
    return 1.0 / (1.0 + jnp.exp(-x))


def _dot(a, b):
    return jnp.dot(a, b, preferred_element_type=F32)


def _dot_nt(a, b):
    return lax.dot_general(a, b, (((1,), (1,)), ((), ())), preferred_element_type=F32)


def _norm_matmul_kernel(x_ref, g_ref, w_ref, o_ref, *rest, n_tail):
    xn_ref = rest[-1]
    j = pl.program_id(1)

    @pl.when(j == 0)
    def _():
        xn_ref[...] = _rms(x_ref[...], g_ref[...]).astype(BF16)

    res = _dot(xn_ref[...], w_ref[...])
    o_ref[...] = res
    if n_tail:
        tail_ref = rest[0]
        tm = o_ref.shape[0]

        @pl.when(j == pl.num_programs(1) - 1)
        def _():
            for slot in range(n_tail):
                tail_ref[pl.ds(slot, tm, stride=n_tail), :] = res[:, slot * LANES:(slot + 1) * LANES]


def norm_matmul(x, g, w, layer, *, tm, tn, tail_rows=False):
    m, k = x.shape
    n = w.shape[2]
    assert m % tm == 0 and n % tn == 0
    n_tail = tn // LANES if tail_rows else 0
    out_specs = [pl.BlockSpec((tm, tn), lambda i, j: (i, j))]
    out_shape = [jax.ShapeDtypeStruct((m, n), F32)]
    if tail_rows:
        out_specs.append(pl.BlockSpec((tm * n_tail, LANES), lambda i, j: (i, 0)))
        out_shape.append(jax.ShapeDtypeStruct((m * n_tail, LANES), F32))
    out = pl.pallas_call(
        functools.partial(_norm_matmul_kernel, n_tail=n_tail),
        grid=(m // tm, n // tn),
        in_specs=[pl.BlockSpec((tm, k), lambda i, j: (i, 0)),
                  pl.BlockSpec((1, k), lambda i, j: (0, 0)),
                  pl.BlockSpec((None, k, tn), lambda i, j: (layer, 0, j))],
        out_specs=out_specs,
        out_shape=out_shape,
        scratch_shapes=[pltpu.VMEM((tm, k), BF16)],
        compiler_params=_cparams(("parallel", "arbitrary"), 56),
        name="norm_matmul",
    )(x, g.reshape(1, k), w)
    return out if tail_rows else out[0]


def _norm_swiglu_kernel(x_ref, g_ref, wg_ref, wu_ref, o_ref, xn_ref):
    @pl.when(pl.program_id(1) == 0)
    def _():
        xn_ref[...] = _rms(x_ref[...], g_ref[...]).astype(BF16)

    xn = xn_ref[...]
    gate = _dot(xn, wg_ref[...].astype(BF16))
    up = _dot(xn, wu_ref[...].astype(BF16))
    o_ref[...] = (gate * _sigmoid(gate) * up).astype(o_ref.dtype)


def norm_swiglu(x, g, w_gate_up, layer, *, tm, tn):
    m, k = x.shape
    dff = w_gate_up.shape[2] // 2
    assert m % tm == 0 and dff % tn == 0
    nj = dff // tn
    return pl.pallas_call(
        _norm_swiglu_kernel,
        grid=(m // tm, nj),
        in_specs=[pl.BlockSpec((tm, k), lambda i, j: (i, 0)),
                  pl.BlockSpec((1, k), lambda i, j: (0, 0)),
                  pl.BlockSpec((None, k, tn), lambda i, j: (layer, 0, j)),
                  pl.BlockSpec((None, k, tn), lambda i, j: (layer, 0, j + nj))],
        out_specs=pl.BlockSpec((tm, tn), lambda i, j: (i, j)),
        out_shape=jax.ShapeDtypeStruct((m, dff), BF16),
        scratch_shapes=[pltpu.VMEM((tm, k), BF16)],
        compiler_params=_cparams(("parallel", "arbitrary"), 56),
        name="norm_swiglu",
    )(x, g.reshape(1, k), w_gate_up, w_gate_up)


def _matmul_postnorm_kernel(*refs, nk, off, has_tail, split_out):
    refs = list(refs)
    y_ref = refs.pop(0)
    yt_ref = refs.pop(0) if has_tail else None
    w_ref, g_ref, h_ref, o_ref = refs[:4]
    ot_ref = refs[4] if split_out else None
    acc_ref = refs[-1]
    k = pl.program_id(1)
    last_tile = pl.program_id(0) == pl.num_programs(0) - 1

    def body(last):
        def part():
            if last and has_tail:
                y = jnp.concatenate([y_ref[:off, :], yt_ref[...]], axis=0)
            else:
                y = y_ref[...]
            return _dot(y, w_ref[...])

        def finish(acc):
            res = h_ref[...] + _rms(acc, g_ref[...])
            o_ref[...] = res
            if last and split_out:
                ot_ref[...] = res[off:]

        if nk == 1:
            finish(part())
            return

        @pl.when(k == 0)
        def _():
            acc_ref[...] = part()

        @pl.when(jnp.logical_and(k > 0, k < nk - 1))
        def _():
            acc_ref[...] += part()

        @pl.when(k == nk - 1)
        def _():
            finish(acc_ref[...] + part())

    if not (has_tail or split_out):
        body(False)
        return
    pl.when(jnp.logical_not(last_tile))(lambda: body(False))
    pl.when(last_tile)(lambda: body(True))


def matmul_postnorm_residual(y, w, layer, g, h, *, tm, y_tail=None, split_lead=None):
    m, n = h.shape
    kdim = y.shape[1]
    tk = kdim if kdim <= 2048 else _col_tile(kdim, (1408, 1024, 512, 256, 128))
    assert m % tm == 0 and kdim % tk == 0
    nk = kdim // tk
    ni = m // tm
    has_tail = y_tail is not None
    split_out = split_lead is not None
    lead = y.shape[0] if has_tail else (split_lead if split_out else m)
    off = lead - (ni - 1) * tm
    if has_tail or split_out:
        assert 0 < off < tm
        assert not has_tail or (y_tail.shape[0] == tm - off and lead + y_tail.shape[0] == m)
        assert not split_out or split_lead == lead
    in_specs = [pl.BlockSpec((tm, tk), lambda i, k: (i, k))]
    args = [y]
    if has_tail:
        in_specs.append(pl.BlockSpec((tm - off, tk), lambda i, k: (0, k)))
        args.append(y_tail)
    in_specs += [pl.BlockSpec((None, tk, n), lambda i, k: (layer, k, 0)),
                 pl.BlockSpec((1, n), lambda i, k: (0, 0)),
                 pl.BlockSpec((tm, n), lambda i, k: (i, 0))]
    args += [w, g.reshape(1, n), h]
    if split_out:
        out_specs = [pl.BlockSpec((tm, n), lambda i, k: (i, 0)), pl.BlockSpec((tm - off, n), lambda i, k: (0, 0))]
        out_shape = [jax.ShapeDtypeStruct((lead, n), F32), jax.ShapeDtypeStruct((tm - off, n), F32)]
    else:
        out_specs = pl.BlockSpec((tm, n), lambda i, k: (i, 0))
        out_shape = jax.ShapeDtypeStruct((m, n), F32)
    return pl.pallas_call(
        functools.partial(_matmul_postnorm_kernel, nk=nk, off=off, has_tail=has_tail, split_out=split_out),
        grid=(ni, nk),
        in_specs=in_specs,
        out_specs=out_specs,
        out_shape=out_shape,
        scratch_shapes=[pltpu.VMEM((tm, n) if nk > 1 else (8, LANES), F32)],
        compiler_params=_cparams(("parallel", "arbitrary"), 56),
        name="matmul_postnorm_residual",
    )(*args)


def _bucket_np(dist):
    n = np.maximum(dist, 0)
    max_exact = NUM_BUCKETS // 2
    ratio = np.log(np.maximum(n, 1).astype(np.float32) / np.float32(max_exact)) / np.float32(
        math.log(MAX_DISTANCE / max_exact))
    large = np.minimum(max_exact + (ratio * (NUM_BUCKETS - max_exact)).astype(np.int32), NUM_BUCKETS - 1)
    return np.where(n < max_exact, n, large).astype(np.int32)


def _bias_tile_kernel(tab_ref, bucket_ref, mask_ref, o_ref, *, unit):
    c = pl.program_id(0)
    b = bucket_ref[...]
    acc = jnp.zeros(b.shape, F32)
    for k in range(NUM_BUCKETS):
        acc = jnp.where(b == k, tab_ref[k, c] * unit, acc)
    o_ref[0] = jnp.where(mask_ref[...] > 0, acc, NEG)


def bias_tiles(table, dist, valid, unit=1.0):
    ncol = table.shape[1]
    r, c = dist.shape
    return pl.pallas_call(
        functools.partial(_bias_tile_kernel, unit=unit),
        grid=(ncol,),
        in_specs=[pl.BlockSpec(memory_space=pltpu.SMEM),
                  pl.BlockSpec((r, c), lambda i: (0, 0)),
                  pl.BlockSpec((r, c), lambda i: (0, 0))],
        out_specs=pl.BlockSpec((1, r, c), lambda i: (i, 0, 0)),
        out_shape=jax.ShapeDtypeStruct((ncol, r, c), F32),
        compiler_params=_cparams(("parallel",), 32),
        name="bias_tiles",
    )(table, jnp.asarray(_bucket_np(dist)), jnp.asarray(valid.astype(np.int32)))


def _band_dist(t):
    return np.arange(t)[:, None] + t - np.arange(2 * t)[None, :]


def _decode_dist(tdec, n_past, cols):
    t = np.arange(8)[:, None]
    k = np.arange(cols)[None, :]
    dist = np.where(k < n_past, t + n_past - k, t - (k - n_past))
    inside = (k < n_past + tdec) & (t < tdec)
    return dist, inside


def _diff_lambda(lq_ref, lk_ref, lam_init):
    e = jnp.exp(jnp.sum(lq_ref[...] * lk_ref[...], axis=-1, keepdims=True))
    return e[0:1] - e[1:2] + lam_init


def _split_maps(q):
    lane = lax.broadcasted_iota(jnp.int32, q.shape, 1)
    return jnp.where(lane < HALF, q, 0.0), jnp.where(lane >= HALF, q, 0.0)


def _attn_a_prompt_kernel(tab_ref, lq_ref, lk_ref, q_ref, k_ref, v_ref, bias_ref, sg_ref, o_ref,
                          k_scr, v_scr, s_scr, m_scr, acc_scr, *, lam_init, scale):
    kvh = pl.program_id(0)
    qi = pl.program_id(2)
    tq = q_ref.shape[0]
    nt = k_scr.shape[0]
    ng = GROUP_A
    nrg = 2 * ng

    @pl.when(qi == 0)
    def _():
        ones = jnp.ones((tq, LANES), BF16)
        for t in range(nt):
            k_scr[t] = k_ref[t * tq:(t + 1) * tq, :].astype(BF16)
            v_scr[t] = jnp.concatenate([v_ref[t * tq:(t + 1) * tq, :].astype(BF16), ones], axis=1)

    q = q_ref[...] * (scale * LOG2E)
    parts1, parts2 = [], []
    for g in range(ng):
        q1, q2 = _split_maps(q[:, g * LANES:(g + 1) * LANES])
        parts1.append(q1)
        parts2.append(q2)
    qs = jnp.concatenate(parts1 + parts2, axis=0).astype(BF16)

    def score_tile(t, bias_of_group, first=False):
        s = _dot_nt(qs, k_scr[t])
        for rg in range(nrg):
            rows = slice(rg * tq, (rg + 1) * tq)
            sg = s[rows] + bias_of_group(rg)
            s_scr[t, rows, :] = sg
            mx = sg[:, :LANES]
            for c in range(1, tq // LANES):
                mx = jnp.maximum(mx, sg[:, c * LANES:(c + 1) * LANES])
            m_scr[rows, :] = mx if first else jnp.maximum(m_scr[rows, :], mx)

    def col_of_group(rg):
        m, g = divmod(rg, ng)
        return 2 * g + m

    score_tile(qi, lambda rg: bias_ref[col_of_group(rg), :, tq:2 * tq], first=True)

    def far_body(t, carry):
        score_tile(t, lambda rg: tab_ref[NUM_BUCKETS - 1, kvh * nrg + col_of_group(rg)] * LOG2E)
        return carry

    lax.fori_loop(0, qi - 1, far_body, 0)

    @pl.when(qi >= 1)
    def _():
        score_tile(qi - 1, lambda rg: bias_ref[col_of_group(rg), :, 0:tq])

    m_row = jnp.max(m_scr[...], axis=-1, keepdims=True)

    def pv_tile(t, first=False):
        p = jnp.exp2(s_scr[t] - m_row)
        pv = _dot(p.astype(BF16), v_scr[t])
        if first:
            acc_scr[...] = pv
        else:
            acc_scr[...] += pv

    pv_tile(qi, first=True)

    def pv_body(t, carry):
        pv_tile(t)
        return carry

    lax.fori_loop(0, qi, pv_body, 0)

    acc = acc_scr[...]
    o_all = acc[:, :LANES] / acc[:, LANES:]
    half = ng * tq
    lam = _diff_lambda(lq_ref, lk_ref, lam_init)
    o = o_all[:half] - lam * o_all[half:]
    for g in range(ng):
        og = _rms(o[g * tq:(g + 1) * tq], sg_ref[...]) * (1.0 - lam_init)
        o_ref[:, g * LANES:(g + 1) * LANES] = og.astype(o_ref.dtype)


def attn_a_prompt(qkv, nb, seq, table, bias_near, lam_q, lam_k, subln_g, lam_init):
    tq = TQ_A
    nq = seq // tq
    nqb = (N_KV_A * GROUP_A * LANES) // LANES
    kern = functools.partial(_attn_a_prompt_kernel, lam_init=lam_init, scale=HEAD_DIM_A ** -0.5)
    nrow = 2 * GROUP_A * tq
    return pl.pallas_call(
        kern,
        grid=(N_KV_A, nb, nq),
        in_specs=[pl.BlockSpec(memory_space=pltpu.SMEM),
                  pl.BlockSpec((2, HEAD_DIM_A), lambda h, b, i: (0, 0)),
                  pl.BlockSpec((2, HEAD_DIM_A), lambda h, b, i: (0, 0)),
                  pl.BlockSpec((tq, GROUP_A * LANES), lambda h, b, i: (b * nq + i, h)),
                  pl.BlockSpec((seq, LANES), lambda h, b, i: (b, nqb + h)),
                  pl.BlockSpec((seq, LANES), lambda h, b, i: (b, nqb + N_KV_A + h)),
                  pl.BlockSpec((2 * GROUP_A, tq, 2 * tq), lambda h, b, i: (h, 0, 0)),
                  pl.BlockSpec((1, LANES), lambda h, b, i: (0, 0))],
        out_specs=pl.BlockSpec((tq, GROUP_A * LANES), lambda h, b, i: (b * nq + i, h)),
        out_shape=jax.ShapeDtypeStruct((nb * seq, N_KV_A * GROUP_A * LANES), BF16),
        scratch_shapes=[pltpu.VMEM((nq, tq, LANES), BF16),
                        pltpu.VMEM((nq, tq, 2 * LANES), BF16),
                        pltpu.VMEM((nq, nrow, tq), F32),
                        pltpu.VMEM((nrow, LANES), F32),
                        pltpu.VMEM((nrow, 2 * LANES), F32)],
        compiler_params=_cparams(("parallel", "parallel", "arbitrary"), 56),
        name="attn_a_prompt",
    )(table, lam_q, lam_k, qkv, qkv, qkv, bias_near, subln_g.reshape(1, LANES))


def _attn_a_sample_kernel(pt_ref, lq_ref, lk_ref, q_ref, *rest, lam_init, scale, npp):
    page_refs = rest[:npp]
    (knew_ref, bfar_ref, blast_ref, bnew_ref, sg_ref, o_ref, m_scr, l_scr, acc_scr) = rest[npp:]
    ps = pl.program_id(1)
    nsteps = pl.num_programs(1)
    nkv = N_KV_A
    tdec = knew_ref.shape[1]

    @pl.when(ps == 0)
    def _():
        m_scr[...] = jnp.full(m_scr.shape, -jnp.inf, F32)
        l_scr[...] = jnp.zeros(l_scr.shape, F32)
        acc_scr[...] = jnp.zeros(acc_scr.shape, F32)

    qs32 = []
    for h in range(nkv):
        q1, q2 = _split_maps(q_ref[0, h] * scale)
        qs32.append(jnp.concatenate([q1, q2], axis=0))
    qs = [x.astype(BF16) for x in qs32]
    is_last = ps == nsteps - 1

    nslot = 2 * nkv
    page = page_refs[0].shape[0] // nslot

    def slot_rows(slot, refs):
        rows = [r[pl.ds(slot, page, stride=nslot), :] for r in refs]
        return jnp.concatenate(rows, axis=0).astype(BF16)

    grp = min(PAGE_GROUP, npp)
    for g0 in range(0, npp, grp):
        refs = page_refs[g0:g0 + grp]
        ps_list, alphas = [], []
        for h in range(nkv):
            kh = slot_rows(h, refs)
            s = _dot_nt(qs[h], kh)
            bfar = bfar_ref[h]
            if g0 + grp == npp:
                tail = jnp.where(is_last, blast_ref[h], bfar)
                s = jnp.concatenate([s[:, :-page] + bfar[:, 0:1], s[:, -page:] + tail], axis=1)
            else:
                s = s + bfar[:, 0:1]
            m_prev = m_scr[h]
            m_new = jnp.maximum(m_prev, jnp.max(s, axis=-1, keepdims=True))
            alpha = jnp.exp(m_prev - m_new)
            p = jnp.exp(s - m_new[:, 0:1])
            l_scr[h] = alpha * l_scr[h] + jnp.sum(p, axis=-1, keepdims=True)
            m_scr[h] = m_new
            ps_list.append(p.astype(BF16))
            alphas.append(alpha)
        for h in range(nkv):
            vh = slot_rows(nkv + h, refs)
            acc_scr[h] = alphas[h] * acc_scr[h] + _dot(ps_list[h], vh)

    @pl.when(is_last)
    def _():
        lam = _diff_lambda(lq_ref, lk_ref, lam_init)
        knew = knew_ref[0]
        for h in range(nkv):
            kn = knew[:, h * LANES:(h + 1) * LANES]
            vn = knew[:, (nkv + h) * LANES:(nkv + h + 1) * LANES]
            bnew = bnew_ref[h]
            cols = [jnp.sum(qs32[h] * kn[t:t + 1, :], axis=-1, keepdims=True) + bnew[:, t:t + 1]
                    for t in range(tdec)]
            m_prev = m_scr[h]
            m_new = m_prev
            for c in cols:
                m_new = jnp.maximum(m_new, c)
            alpha = jnp.exp(m_prev - m_new)
            l = alpha * l_scr[h]
            acc = alpha * acc_scr[h]
            for t in range(tdec):
                p = jnp.exp(cols[t] - m_new)
                l = l + p
                acc = acc + p * vn[t:t + 1, :]
            o_all = acc / l
            half = o_all.shape[0] // 2
            o = o_all[:half] - lam * o_all[half:]
            o_ref[0, h] = (_rms(o, sg_ref[...]) * (1.0 - lam_init)).astype(o_ref.dtype)


def attn_a_sample(q_r, cache, layer, page_table, knew, bias_far, bias_last, bias_new,
                  lam_q, lam_k, subln_g, lam_init):
    nb, nkv, ngt, _ = q_r.shape
    n_pages = page_table.shape[1]
    page_rows = cache.shape[2]
    assert page_rows == 2 * nkv * LANES and cache.shape[3] == LANES
    npp = min(PAGES_PER_STEP, n_pages)
    assert n_pages % npp == 0
    tdec = knew.shape[1]
    width = knew.shape[2]
    kern = functools.partial(_attn_a_sample_kernel, lam_init=lam_init, scale=HEAD_DIM_A ** -0.5, npp=npp)

    def page_spec(i):
        return pl.BlockSpec((None, None, page_rows, LANES),
                            lambda b, p, pt: (layer, pt[b, p * npp + i], 0, 0))

    const3 = lambda b, p, pt: (0, 0, 0)
    grid_spec = pltpu.PrefetchScalarGridSpec(
        num_scalar_prefetch=1,
        grid=(nb, n_pages // npp),
        in_specs=[pl.BlockSpec((2, HEAD_DIM_A), lambda b, p, pt: (0, 0)),
                  pl.BlockSpec((2, HEAD_DIM_A), lambda b, p, pt: (0, 0)),
                  pl.BlockSpec((1, nkv, ngt, LANES), lambda b, p, pt: (b, 0, 0, 0))]
                 + [page_spec(i) for i in range(npp)]
                 + [pl.BlockSpec((1, tdec, width), lambda b, p, pt: (b, 0, 0)),
                    pl.BlockSpec((nkv, 2 * ngt, LANES), const3),
                    pl.BlockSpec((nkv, 2 * ngt, LANES), const3),
                    pl.BlockSpec((nkv, 2 * ngt, LANES), const3),
                    pl.BlockSpec((1, LANES), lambda b, p, pt: (0, 0))],
        out_specs=pl.BlockSpec((1, nkv, ngt, LANES), lambda b, p, pt: (b, 0, 0, 0)),
        scratch_shapes=[pltpu.VMEM((nkv, 2 * ngt, LANES), F32),
                        pltpu.VMEM((nkv, 2 * ngt, LANES), F32),
                        pltpu.VMEM((nkv, 2 * ngt, LANES), F32)],
    )
    return pl.pallas_call(
        kern,
        grid_spec=grid_spec,
        out_shape=jax.ShapeDtypeStruct((nb, nkv, ngt, LANES), BF16),
        compiler_params=_cparams(("parallel", "arbitrary"), 56),
        name="attn_a_sample",
    )(page_table, lam_q, lam_k, q_r, *([cache] * npp), knew, bias_far, bias_last, bias_new,
      subln_g.reshape(1, LANES))


def _dup_half(x, odd):
    lane = lax.broadcasted_iota(jnp.int32, x.shape, 1)
    rolled = pltpu.roll(x, HALF, axis=1)
    keep = (lane >= HALF) if odd else (lane < HALF)
    return jnp.where(keep, x, rolled)


def _swa_prompt_kernel(sink_ref, q_ref, kvp_ref, kvo_ref, bias_ref, o_ref, *, scale):
    n = pl.program_id(1)
    w = q_ref.shape[0]
    ng = GROUP_C
    lane = lax.broadcasted_iota(jnp.int32, (w, LANES), 1)
    kcol = lax.broadcasted_iota(jnp.int32, (2 * w, 2 * w), 1)
    first = n == 0
    vpair0 = (N_KV_C * HEAD_DIM_C) // LANES
    ones = jnp.ones((2 * w, LANES), BF16)

    def kv_pair(c):
        cols = slice(c * LANES, (c + 1) * LANES)
        return jnp.concatenate([kvp_ref[:, cols], kvo_ref[:, cols]], axis=0)

    for h in range(N_KV_C):
        kd = _dup_half(kv_pair(h // 2), h % 2).astype(BF16)
        vd = _dup_half(kv_pair(vpair0 + h // 2), h % 2).astype(BF16)
        vd1 = jnp.concatenate([vd, ones], axis=1)
        for j in range(ng // 2):
            hd = h * ng + 2 * j
            col = (hd // 2) * LANES
            qp = q_ref[:, col:col + LANES] * scale
            qs = jnp.concatenate([jnp.where(lane < HALF, qp, 0.0), jnp.where(lane >= HALF, qp, 0.0)], axis=0)
            sink = jnp.concatenate([jnp.full((w, LANES), sink_ref[hd], F32),
                                    jnp.full((w, LANES), sink_ref[hd + 1], F32)], axis=0)
            bias = bias_ref[hd:hd + 2].reshape(2 * w, 2 * w)
            bias = jnp.where(jnp.logical_and(first, kcol < w), NEG, bias)
            s = _dot_nt(qs.astype(BF16), kd) + bias
            m = jnp.maximum(jnp.max(s, axis=-1, keepdims=True), sink)
            p = jnp.concatenate([jnp.exp(s[:, c * LANES:(c + 1) * LANES] - m) for c in range(2 * w // LANES)],
                                axis=1)
            ov = _dot(p.astype(BF16), vd1)
            o = ov[:, :LANES] / (ov[:, LANES:] + jnp.exp(sink - m))
            o_ref[:, col:col + LANES] = jnp.where(lane < HALF, o[:w], o[w:]).astype(o_ref.dtype)


def swa_prompt(qkv, nb, seq, sinks, bias_swa):
    w = WINDOW
    nblk = seq // w
    dq = N_KV_C * GROUP_C * HEAD_DIM_C
    dkv = 2 * N_KV_C * HEAD_DIM_C
    kern = functools.partial(_swa_prompt_kernel, scale=HEAD_DIM_C ** -0.5)
    return pl.pallas_call(
        kern,
        grid=(nb, nblk),
        in_specs=[pl.BlockSpec(memory_space=pltpu.SMEM),
                  pl.BlockSpec((w, dq), lambda b, n: (b * nblk + n, 0)),
                  pl.BlockSpec((w, dkv), lambda b, n: (b * nblk + jnp.maximum(n - 1, 0), dq // dkv)),
                  pl.BlockSpec((w, dkv), lambda b, n: (b * nblk + n, dq // dkv)),
                  pl.BlockSpec((N_KV_C * GROUP_C, w, 2 * w), lambda b, n: (0, 0, 0))],
        out_specs=pl.BlockSpec((w, dq), lambda b, n: (b * nblk + n, 0)),
        out_shape=jax.ShapeDtypeStruct((nb * seq, dq), BF16),
        compiler_params=_cparams(("parallel", "parallel"), 48),
        name="swa_prompt",
    )(sinks, qkv, qkv, qkv, bias_swa)


def _swa_sample_kernel(q_ref, k_ref, v_ref, bias_ref, sink_ref, o_ref, *, scale):
    for h in range(N_KV_C):
        q = (q_ref[0, h] * scale).astype(BF16)
        s = _dot_nt(q, k_ref[0, h].astype(BF16)) + bias_ref[h]
        sink = sink_ref[h][:, 0:1]
        m = jnp.maximum(jnp.max(s, axis=-1, keepdims=True), sink)
        p = jnp.exp(s - m)
        den = jnp.sum(p, axis=-1, keepdims=True) + jnp.exp(sink - m)
        o_ref[0, h] = (_dot(p.astype(BF16), v_ref[0, h].astype(BF16)) / den).astype(o_ref.dtype)


def swa_sample(q_r, k_r, v_r, bias, sink_rows):
    nb, nkv, ngt, d = q_r.shape
    keys = k_r.shape[2]
    kern = functools.partial(_swa_sample_kernel, scale=HEAD_DIM_C ** -0.5)
    return pl.pallas_call(
        kern,
        grid=(nb,),
        in_specs=[pl.BlockSpec((1, nkv, ngt, d), lambda b: (b, 0, 0, 0)),
                  pl.BlockSpec((1, nkv, keys, d), lambda b: (b, 0, 0, 0)),
                  pl.BlockSpec((1, nkv, keys, d), lambda b: (b, 0, 0, 0)),
                  pl.BlockSpec((nkv, ngt, keys), lambda b: (0, 0, 0)),
                  pl.BlockSpec((nkv, ngt, LANES), lambda b: (0, 0, 0))],
        out_specs=pl.BlockSpec((1, nkv, ngt, d), lambda b: (b, 0, 0, 0)),
        out_shape=jax.ShapeDtypeStruct((nb, nkv, ngt, d), BF16),
        compiler_params=_cparams(("parallel",), 32),
        name="swa_sample",
    )(q_r, k_r, v_r, bias, sink_rows)


def _lower_bound(lbp_ref, layer):
    x = lbp_ref[...]
    e = jnp.exp(x - jnp.max(x, axis=0, keepdims=True))
    den = jnp.sum(e, axis=0, keepdims=True)
    num = e[1:2]
    for l in range(2, layer + 1):
        num = num + e[l:l + 1]
    if layer == 0:
        num = jnp.zeros_like(den)
    return num / den


def _cumsum_rows(x):
    rows = x.shape[0]
    row = lax.broadcasted_iota(jnp.int32, x.shape, 0)
    sh = 1
    while sh < rows:
        x = x + jnp.where(row >= sh, pltpu.roll(x, sh, axis=0), 0.0)
        sh *= 2
    return x


def _gla_gates(qr, fr, lb):
    q = qr * _sigmoid(qr)
    f = lb + (1.0 - lb) * _sigmoid(fr)
    return q, 1.0 - f, jnp.log2(f)


def _gla_scores(q, k, bcum, sub):
    rows = q.shape[0]
    sl = 8
    srow = lax.broadcasted_iota(jnp.int32, (sl, LANES), 0)
    lane = lax.broadcasted_iota(jnp.int32, (sl, LANES), 1)
    out = []
    for i in range(rows // sub):
        lo = i * sub
        qi, ki, bi = q[lo:lo + sub], k[lo:lo + sub], bcum[lo:lo + sub]
        a = jnp.zeros((sub, LANES), F32)
        if i > 0:
            ref = bcum[lo - 1:lo]
            qd = qi * jnp.exp2(bi - ref)
            kd = jnp.concatenate([k[:lo] * jnp.exp2(ref - bcum[:lo]), jnp.zeros((rows - lo, LANES), F32)], axis=0)
            a = _dot_nt(qd.astype(BF16), kd.astype(BF16))
            if rows < LANES:
                a = jnp.concatenate([a, jnp.zeros((sub, LANES - rows), F32)], axis=1)
        slabs = [a[r:r + sl] for r in range(0, sub, sl)]
        for s in range(sub):
            for n in range(s // sl, sub // sl):
                r = n * sl
                d = bi[r:r + sl] - bi[s:s + 1]
                if n == s // sl:
                    d = jnp.where(srow + r >= s, d, -jnp.inf)
                col = jnp.sum(qi[r:r + sl] * jnp.exp2(d) * ki[s:s + 1], axis=-1, keepdims=True)
                slabs[n] = jnp.where(lane == lo + s, col, slabs[n])
        out.extend(slabs)
    return out[0] if len(out) == 1 else jnp.concatenate(out, axis=0)


def _pad_rows(x, rows):
    if x.shape[0] == rows:
        return x
    return jnp.concatenate([x, jnp.zeros((rows - x.shape[0], x.shape[1]), x.dtype)], axis=0)


def _gla_chunk(st, q, k, v, g, sub, last):
    bcum = _cumsum_rows(g)
    b_last = bcum[last:last + 1]
    o = _dot_nt((q * jnp.exp2(bcum)).astype(BF16), st.astype(BF16))
    a = _gla_scores(q, k, bcum, sub)
    v128 = _pad_rows(v, LANES).astype(BF16)
    o = o + _dot(a.astype(BF16), v128)
    kdec = _pad_rows(k * jnp.exp2(b_last - bcum), LANES).astype(BF16)
    vt = _pad_rows(v, LANES).T.astype(BF16)
    st_new = st * jnp.exp2(b_last) + _dot(vt, kdec)
    return o, st_new


def _gla_finish(o, gate, ng_ref):
    return (_rms(o, ng_ref[...]) * (gate * _sigmoid(gate))).astype(BF16)


def _hgrn_prompt_kernel(lbp_ref, hq_ref, hf_ref, hi_ref, hg_ref, ng_ref, y_ref, sfin_ref, st_scr, *, layer):
    r = pl.program_id(2)
    c = CHUNK_B

    @pl.when(r == 0)
    def _():
        st_scr[...] = jnp.zeros(st_scr.shape, F32)

    nhs = st_scr.shape[0]
    lbs = [_lower_bound(lbp_ref.at[:, hd * LANES:(hd + 1) * LANES], layer) for hd in range(nhs)]

    def body(ci, carry):
        r0 = pl.multiple_of(ci * c, c)
        rows = pl.ds(r0, c)
        outs = []
        for hd in range(nhs):
            cols = slice(hd * LANES, (hd + 1) * LANES)
            q, k, g = _gla_gates(hq_ref[rows, cols], hf_ref[rows, cols], lbs[hd])
            o, st_new = _gla_chunk(st_scr[hd], q, k, hi_ref[rows, cols], g, SUB_B, c - 1)
            st_scr[hd] = st_new
            outs.append(o)
        for hd in range(nhs):
            cols = slice(hd * LANES, (hd + 1) * LANES)
            y_ref[rows, cols] = _gla_finish(outs[hd], hg_ref[rows, cols], ng_ref)
        return carry

    lax.fori_loop(0, hq_ref.shape[0] // c, body, 0)

    @pl.when(r == pl.num_programs(2) - 1)
    def _():
        for hd in range(nhs):
            sfin_ref[0, hd] = st_scr[hd].T


def hgrn_prompt(h_in, nb, seq, lb_param, norm_g, layer):
    nh = h_in.shape[1] // (4 * LANES)
    nhs = HEADS_B
    assert nh % nhs == 0
    ng = nh // nhs
    rows = min(ROWS_B, seq)
    nr = seq // rows
    depth = lb_param.shape[0]
    kern = functools.partial(_hgrn_prompt_kernel, layer=layer)

    def col(j):
        return pl.BlockSpec((rows, nhs * LANES), lambda b, h, r: (b * nr + r, j * ng + h))

    return pl.pallas_call(
        kern,
        grid=(nb, ng, nr),
        in_specs=[pl.BlockSpec((depth, nhs * LANES), lambda b, h, r: (0, h)),
                  col(0), col(1), col(2), col(3),
                  pl.BlockSpec((1, LANES), lambda b, h, r: (0, 0))],
        out_specs=[pl.BlockSpec((rows, nhs * LANES), lambda b, h, r: (b * nr + r, h)),
                   pl.BlockSpec((1, nhs, LANES, LANES), lambda b, h, r: (b, h, 0, 0))],
        out_shape=[jax.ShapeDtypeStruct((nb * seq, nh * LANES), BF16),
                   jax.ShapeDtypeStruct((nb, nh, LANES, LANES), F32)],
        scratch_shapes=[pltpu.VMEM((nhs, LANES, LANES), F32)],
        compiler_params=_cparams(("parallel", "parallel", "arbitrary"), 32),
        name="hgrn_prompt",
    )(lb_param, h_in, h_in, h_in, h_in, norm_g.reshape(1, LANES))


def _hgrn_sample_kernel(lbp_ref, h_ref, ng_ref, s0_ref, y_ref, s1_ref, *, layer, tdec):
    nh = s0_ref.shape[1]
    rows = h_ref.shape[1]
    for hd in range(nh):
        def cols(j, hd=hd):
            return slice((j * nh + hd) * LANES, (j * nh + hd + 1) * LANES)
        lb = _lower_bound(lbp_ref.at[:, cols(0)], layer)
        q, k, g = _gla_gates(h_ref[0, :, cols(0)], h_ref[0, :, cols(1)], lb)
        o, st_new = _gla_chunk(s0_ref[0, hd].T, q, k, h_ref[0, :, cols(2)], g, rows, tdec - 1)
        y_ref[0, :, cols(0)] = _gla_finish(o, h_ref[0, :, cols(3)], ng_ref)
        s1_ref[0, hd] = st_new.T


def hgrn_sample(h_in, state, state_layer, lb_param, norm_g, layer, tdec):
    nb, rows, width = h_in.shape
    nh = width // (4 * LANES)
    depth = lb_param.shape[0]
    kern = functools.partial(_hgrn_sample_kernel, layer=layer, tdec=tdec)
    return pl.pallas_call(
        kern,
        grid=(nb,),
        in_specs=[pl.BlockSpec((depth, nh * LANES), lambda b: (0, 0)),
                  pl.BlockSpec((1, rows, width), lambda b: (b, 0, 0)),
                  pl.BlockSpec((1, LANES), lambda b: (0, 0)),
                  pl.BlockSpec((None, 1, nh, LANES, LANES), lambda b: (state_layer, b, 0, 0, 0))],
        out_specs=[pl.BlockSpec((1, rows, nh * LANES), lambda b: (b, 0, 0)),
                   pl.BlockSpec((1, nh, LANES, LANES), lambda b: (b, 0, 0, 0))],
        out_shape=[jax.ShapeDtypeStruct((nb, rows, nh * LANES), BF16),
                   jax.ShapeDtypeStruct((nb, nh, LANES, LANES), F32)],
        compiler_params=_cparams(("parallel",), 32),
        name="hgrn_sample",
    )(lb_param, h_in, norm_g.reshape(1, LANES), state)


def _row_tile(m, cands=(640, 512, 256, 128, 64, 32, 16, 8)):
    for t in cands:
        if m % t == 0:
            return t
    raise ValueError(f"unsupported row count {m}")


def _col_tile(n, cands):
    for t in cands:
        if n % t == 0:
            return t
    raise ValueError(f"unsupported column count {n}")


def kernel(x_prompt, x_sample, cache_kv_a, state_hgrn_b, cache_win_c, page_table, norm_g, a_w_in, a_w_out, a_lambda, a_subln_g, b_w_in, b_w_out, b_lower_bound, b_norm_g, c_w_in, c_w_out, c_sinks, rel_bias_table, ffn_w_gate_up, ffn_w_down):
    nb, seq, d = x_prompt.shape
    nbs, tdec, _ = x_sample.shape
    depth = norm_g.shape[0]
    mp, ms = nb * seq, nbs * tdec
    m = mp + ms
    tm = _row_tile(m)
    tm_in = _row_tile(m, (1040, 640, 512, 256, 128, 64, 32, 16, 8))
    straddle = ms < tm and mp % tm == tm - ms
    page = cache_kv_a.shape[2]
    assert page == LANES and page >= MAX_DISTANCE and tdec <= 8
    kvw = N_KV_A * 2 * HEAD_DIM_A
    cache = cache_kv_a.reshape(cache_kv_a.shape[0], cache_kv_a.shape[1], page * 2 * N_KV_A, 2 * HEAD_DIM_A)
    table = rel_bias_table.astype(F32)
    ga, gc = GROUP_A, GROUP_C
    wb = cache_win_c.shape[2]
    keys_c = 2 * WINDOW

    dist_a = _band_dist(TQ_A)
    bias_a = bias_tiles(table, dist_a, dist_a >= 0, unit=LOG2E)
    dist_c = _band_dist(WINDOW)
    bias_c = bias_tiles(table, dist_c, (dist_c >= 0) & (dist_c <= WINDOW))

    dist_as, in_as = _decode_dist(tdec, page, 2 * LANES)
    bias_as = bias_tiles(table, dist_as, in_as & (dist_as >= 0))
    bias_as = bias_as.reshape(N_KV_A, ga, 2, 8, 2 * LANES).transpose(0, 2, 1, 3, 4)[:, :, :, :tdec]
    bias_as = bias_as.reshape(N_KV_A, 2 * ga * tdec, 2 * LANES)
    last_a, new_a = bias_as[:, :, :LANES], bias_as[:, :, LANES:]
    far_a = table[NUM_BUCKETS - 1].reshape(N_KV_A, ga, 2).transpose(0, 2, 1)
    far_a = jnp.broadcast_to(far_a[:, :, :, None, None], (N_KV_A, 2, ga, tdec, LANES))
    far_a = far_a.reshape(N_KV_A, 2 * ga * tdec, LANES)
    dist_cs, in_cs = _decode_dist(tdec, wb, keys_c)
    bias_cs = bias_tiles(table, dist_cs, in_cs & (dist_cs >= 0) & (dist_cs <= WINDOW))
    bias_cs = bias_cs.reshape(N_KV_C, gc, 8, keys_c)[:, :, :tdec].reshape(N_KV_C, gc * tdec, keys_c)

    a_w_in16, a_w_out16 = a_w_in.astype(BF16), a_w_out.astype(BF16)
    b_w_in16, b_w_out16 = b_w_in.astype(BF16), b_w_out.astype(BF16)
    c_w_in16, c_w_out16 = c_w_in.astype(BF16), c_w_out.astype(BF16)
    ffn_w_down16 = ffn_w_down.astype(BF16)
    dff = ffn_w_gate_up.shape[2] // 2

    h = jnp.concatenate([x_prompt.reshape(mp, d), x_sample.reshape(ms, d)], axis=0)
    kv_p, kv_s, hg_p, hg_s, win_p, win_s = [], [], [], [], [], []
    for layer in range(depth):
        kind, j = layer % 3, layer // 3
        g = norm_g[layer]
        if kind == 0:
            lam_init = 0.8 - 0.6 * math.exp(-0.3 * layer)
            nslot = 2 * kvw // LANES
            qkv, kv_rows = norm_matmul(h, g[0], a_w_in16, j, tm=tm_in, tn=2 * kvw, tail_rows=True)
            dq = N_KV_A * GROUP_A * 2 * HEAD_DIM_A
            lam_q, lam_k = a_lambda[j][0::2], a_lambda[j][1::2]
            yp = attn_a_prompt(qkv, nb, seq, table, bias_a, lam_q, lam_k, a_subln_g[j], lam_init)
            qs = qkv[mp:, :dq].reshape(nbs, tdec, N_KV_A, ga, LANES).transpose(0, 2, 3, 1, 4)
            qs = qs.reshape(nbs, N_KV_A, ga * tdec, LANES)
            kvs = qkv[mp:, dq:].reshape(nbs, tdec, 2 * kvw)
            ys = attn_a_sample(qs, cache, j, page_table, kvs, far_a, last_a, new_a,
                               lam_q, lam_k, a_subln_g[j], lam_init)
            ys = ys.reshape(nbs, N_KV_A, ga, tdec, LANES).transpose(0, 3, 1, 2, 4).reshape(ms, dq)
            kv_p.append(kv_rows[:mp * nslot].reshape(nb, seq, 2, N_KV_A, 2 * HEAD_DIM_A))
            kv_s.append(kv_rows[mp * nslot:].reshape(nbs, tdec, 2, N_KV_A, 2 * HEAD_DIM_A))
            w_out = a_w_out16
        elif kind == 1:
            hin = norm_matmul(h, g[0], b_w_in16, j, tm=tm_in, tn=_col_tile(b_w_in.shape[2], (1024, 512, 256, 128)))
            yp, sp = hgrn_prompt(hin, nb, seq, b_lower_bound, b_norm_g[j], layer)
            hs_in = jnp.pad(hin[mp:].reshape(nbs, tdec, hin.shape[1]), ((0, 0), (0, 8 - tdec), (0, 0)))
            ys, ss = hgrn_sample(hs_in, state_hgrn_b, j, b_lower_bound, b_norm_g[j], layer, tdec)
            ys = ys[:, :tdec].reshape(ms, d)
            hg_p.append(sp)
            hg_s.append(ss)
            w_out = b_w_out16
        else:
            qkv = norm_matmul(h, g[0], c_w_in16, j, tm=tm_in, tn=_col_tile(c_w_in.shape[2], (1280, 512, 256, 128)))
            dq = N_KV_C * GROUP_C * HEAD_DIM_C
            dkv = N_KV_C * HEAD_DIM_C
            yp = swa_prompt(qkv, nb, seq, c_sinks[j], bias_c)
            kv_new = qkv[mp:, dq:].reshape(nbs, tdec, 2 * dkv)
            kv_all = jnp.concatenate([cache_win_c[j].reshape(nbs, wb, 2 * dkv), kv_new], axis=1)
            kv_pad = jnp.pad(kv_all, ((0, 0), (0, keys_c - wb - tdec), (0, 0)))
            kv_r = kv_pad.reshape(nbs, keys_c, 2, N_KV_C, HEAD_DIM_C).transpose(2, 0, 3, 1, 4)
            qs = qkv[mp:, :dq].reshape(nbs, tdec, N_KV_C, gc, HEAD_DIM_C).transpose(0, 2, 3, 1, 4)
            qs = qs.reshape(nbs, N_KV_C, gc * tdec, HEAD_DIM_C)
            sink_rows = jnp.broadcast_to(c_sinks[j].reshape(N_KV_C, gc, 1, 1), (N_KV_C, gc, tdec, LANES))
            sink_rows = sink_rows.reshape(N_KV_C, gc * tdec, LANES)
            ys = swa_sample(qs, kv_r[0], kv_r[1], bias_cs, sink_rows)
            ys = ys.reshape(nbs, N_KV_C, gc, tdec, HEAD_DIM_C).transpose(0, 3, 1, 2, 4).reshape(ms, dq)
            win_p.append(qkv[:mp, dq:].reshape(nb, seq, 2, N_KV_C, HEAD_DIM_C)[:, seq - wb:])
            win_s.append(kv_all[:, tdec:].reshape(nbs, wb, 2, N_KV_C, HEAD_DIM_C))
            w_out = c_w_out16
        if straddle:
            h = matmul_postnorm_residual(yp, w_out, j, g[1], h, tm=tm, y_tail=ys)
        else:
            h = matmul_postnorm_residual(jnp.concatenate([yp, ys], axis=0), w_out, j, g[1], h, tm=tm)
        hm = norm_swiglu(h, g[2], ffn_w_gate_up, layer, tm=tm_in, tn=_col_tile(dff, (512, 256, 128)))
        if straddle and layer == depth - 1:
            hp, hs = matmul_postnorm_residual(hm, ffn_w_down16, layer, g[3], h, tm=tm, split_lead=mp)
        else:
            h = matmul_postnorm_residual(hm, ffn_w_down16, layer, g[3], h, tm=tm)
            hp, hs = h[:mp], h[mp:]
    return (hp.reshape(nb, seq, d), hs.reshape(nbs, tdec, d),
            jnp.stack(kv_p), jnp.stack(kv_s), jnp.stack(hg_p), jnp.stack(hg_s),
            jnp.stack(win_p), jnp.stack(win_s))
```
